```python
import math
import jax, jax.numpy as jnp
from jax import lax
import numpy as np

D_MODEL = 2048
BATCH = 32
SEQ = 256
DEPTH = 2
DEC_BATCH = 8
DEC_SEQ = 2048
PAST_LEN = 512

GRID_W = 64
D_RWKV = D_MODEL // 2
N_RWKV = 64
H_RWKV = D_RWKV // N_RWKV
D_LORA_W = 64
D_LORA_A = 64
D_LORA_G = 160
D_DIFF = D_MODEL // 4
D_HEAD_DIFF = 64
H_DIFF = D_DIFF // (2 * D_HEAD_DIFF)
Q_BLOCK = 128
ROPE_THETA = 10000.0
D_CHUNK = D_MODEL // 4
G_CHUNK = 4
CHUNK = 128
D_MIX = D_RWKV + D_DIFF + D_CHUNK
D_RWKV_IN = 3 * D_RWKV + 2 * D_LORA_W + 2 * D_LORA_A + D_LORA_G
D_IN = D_RWKV_IN + 3 * D_DIFF + 2 * D_CHUNK
N_EXPERTS = 32
TOP_K = 4
D_FF_EXPERT = D_MODEL
SWIGLU_LIMIT = 7.0
SWIGLU_ALPHA = 1.702
MOE_BLOCK = 128
NORM_EPS = 1e-6
GN_EPS = 64e-5

kernel_name = 'hybrid_rwkv7_diffattn_chunkmlp_moe_denoise_step'


def rms_norm(x, g):
    xf = x.astype(jnp.float32)
    y = xf * lax.rsqrt(jnp.mean(xf * xf, axis=-1, keepdims=True) + NORM_EPS)
    return (y * g.astype(jnp.float32)).astype(x.dtype)


def centred_shift(z, mu):
    z_prev = jnp.pad(z[:, :-1], ((0, 0), (1, 0), (0, 0)))
    z_next = jnp.pad(z[:, 1:], ((0, 0), (0, 1), (0, 0)))
    return z + mu[0] * (z_prev - z) + mu[1] * (z_next - z)


def wkv_scan(r, w, k, v, kk, a, s0, reverse):
    def step(s, inp):
        r_t, w_t, k_t, v_t, kk_t, a_t = inp
        sa = jnp.einsum('bhij,bhj->bhi', s, -kk_t)
        s = s * w_t[:, :, None, :] + sa[..., None] * (kk_t * a_t)[:, :, None, :] + v_t[..., None] * k_t[:, :, None, :]
        return s, jnp.einsum('bhij,bhj->bhi', s, r_t)
    xs = tuple(jnp.moveaxis(t.astype(jnp.float32), 1, 0) for t in (r, w, k, v, kk, a))
    s_fin, y = lax.scan(step, s0.astype(jnp.float32), xs, reverse=reverse)
    return jnp.moveaxis(y, 0, 1), s_fin


def rwkv7_mixer(zr, W, l, s0):
    B, S, _ = zr.shape
    heads = lambda t: t.reshape(B, S, H_RWKV, N_RWKV)
    zr = centred_shift(zr, W['shift_mu'][l])
    r, k, v, cw, ca, cg = jnp.split(zr, [D_RWKV, 2 * D_RWKV, 3 * D_RWKV, 3 * D_RWKV + 2 * D_LORA_W, 3 * D_RWKV + 2 * D_LORA_W + 2 * D_LORA_A], axis=-1)
    kk = heads(k * W['k_k'][l]).astype(jnp.float32)
    kk = kk / jnp.maximum(jnp.linalg.norm(kk, axis=-1, keepdims=True), 1e-12)
    g = jax.nn.sigmoid(cg) @ W['gate_g2'][l]
    r, v = heads(r), heads(v)
    ys, bonus, states = [], [], []
    for d in range(2):
        lw = W['decay_w0'][l, d] + jnp.tanh(cw[..., d * D_LORA_W:(d + 1) * D_LORA_W]) @ W['decay_w2'][l, d]
        decay = jnp.exp(-jnp.exp(-jax.nn.softplus(-lw.astype(jnp.float32)) - 0.5))
        a = jax.nn.sigmoid(W['iclr_a0'][l, d] + ca[..., d * D_LORA_A:(d + 1) * D_LORA_A] @ W['iclr_a2'][l, d])
        kd = heads(k * (1 + (a - 1) * W['k_a'][l]))
        init = jnp.zeros((B, H_RWKV, N_RWKV, N_RWKV), jnp.float32) if s0 is None else s0[:, d]
        y, s_fin = wkv_scan(r, heads(decay), kd, v, kk, heads(a), init, reverse=(d == 1))
        ys.append(y)
        states.append(s_fin)
        bonus.append(jnp.sum(r * kd * W['r_k'][l], axis=-1, keepdims=True) * v)
    y = ys[0] + ys[1]
    mu = jnp.mean(y, axis=-1, keepdims=True)
    var = jnp.mean(jnp.square(y - mu), axis=-1, keepdims=True)
    yn = ((y - mu) * lax.rsqrt(var + GN_EPS)).reshape(B, S, D_RWKV) * W['ln_x_w'][l] + W['ln_x_b'][l]
    out = (yn.astype(zr.dtype) + (bonus[0] + bonus[1]).reshape(B, S, D_RWKV)) * g
    return out, jnp.stack(states, axis=1)


def axial_rope(S):
    rows = S // GRID_W
    row = jnp.repeat(jnp.arange(rows), GRID_W)
    col = jnp.tile(jnp.arange(GRID_W), rows)
    half = D_HEAD_DIFF // 2
    inv_freq = 1.0 / (ROPE_THETA ** (jnp.arange(0, half, 2, dtype=jnp.float32) / half))
    ang = jnp.stack([row, col], axis=-1).astype(jnp.float32)[..., None] * inv_freq
    return jnp.cos(ang), jnp.sin(ang)


def apply_rope(x, rope):
    cos, sin = rope
    B, S, H, C, d = x.shape
    xa = x.astype(jnp.float32).reshape(B, S, H, C, 2, 2, d // 4)
    x1, x2 = xa[..., 0, :], xa[..., 1, :]
    c = cos[None, :, None, None]
    s = sin[None, :, None, None]
    out = jnp.stack([x1 * c - x2 * s, x2 * c + x1 * s], axis=-2)
    return out.reshape(x.shape).astype(x.dtype)


def diff_attention(q, k, v, lam):
    B, Sq = q.shape[:2]
    nb = Sq // Q_BLOCK
    qb = jnp.moveaxis(q.reshape(B, nb, Q_BLOCK, H_DIFF, 2, D_HEAD_DIFF), 1, 0)
    scale = D_HEAD_DIFF ** -0.5

    def block(qi):
        s = jnp.einsum('bqhcd,bkhcd->bhcqk', qi, k).astype(jnp.float32) * scale
        p = jax.nn.softmax(s, axis=-1)
        pd = p[:, :, 0] - lam * p[:, :, 1]
        return jnp.einsum('bhqk,bkhe->bqhe', pd.astype(v.dtype), v)

    o = lax.map(block, qb)
    return jnp.moveaxis(o, 0, 1).reshape(B, Sq, H_DIFF, 2 * D_HEAD_DIFF)


def diff_mixer(zq, zk, zv, W, l, rope, ctx_kv):
    B, S, _ = zq.shape
    q = zq.reshape(B, S, H_DIFF, 2, D_HEAD_DIFF)
    k = zk.reshape(B, S, H_DIFF, 2, D_HEAD_DIFF)
    v = zv.reshape(B, S, H_DIFF, 2 * D_HEAD_DIFF)
    lam_init = 0.8 - 0.6 * math.exp(-0.3 * l)
    lq1, lk1, lq2, lk2 = W['diff_lam'][l].astype(jnp.float32)
    lam = jnp.exp(jnp.sum(lq1 * lk1)) - jnp.exp(jnp.sum(lq2 * lk2)) + lam_init
    if ctx_kv is None:
        o = diff_attention(q, k, v, lam)
    else:
        ctx_k, ctx_v = ctx_kv
        keys = jnp.concatenate([apply_rope(k, rope), ctx_k.astype(k.dtype)], axis=1)
        vals = jnp.concatenate([v, ctx_v.astype(v.dtype)], axis=1)
        o = diff_attention(apply_rope(q, rope), keys, vals, lam)
    o = rms_norm(o, W['diff_subln'][l]) * (1 - lam_init)
    return o.reshape(B, S, D_DIFF), k, v


def chunk_mixer(zu, zv, W, l):
    B, S, _ = zu.shape
    u = jax.nn.gelu(zu)
    v = rms_norm(jax.nn.gelu(zv), W['chunk_vnorm'][l])
    vc = v.reshape(B, S // CHUNK, CHUNK, G_CHUNK, D_CHUNK // G_CHUNK)
    s = jnp.einsum('gpq,bnqgc->bnpgc', W['chunk_ws'][l], vc) + W['chunk_bs'][l].T[:, :, None]
    return u * s.reshape(B, S, D_CHUNK)


def moe(h, router_w, router_b, w_gu, b_gu, w_dn, b_dn):
    T, D = h.shape
    logits = (h @ router_w + router_b).astype(jnp.float32)
    top_v, top_i = lax.top_k(logits, TOP_K)
    gates = jax.nn.softmax(top_v, axis=-1)
    TK = T * TOP_K
    flat_e = top_i.reshape(-1)
    flat_tok = jnp.arange(TK, dtype=jnp.int32) // TOP_K
    order = jnp.argsort(flat_e)
    sorted_e = flat_e[order]
    counts = jnp.bincount(flat_e, length=N_EXPERTS)
    padded = ((counts + MOE_BLOCK - 1) // MOE_BLOCK) * MOE_BLOCK
    pad_end = jnp.cumsum(padded)
    pad_start = pad_end - padded
    start = jnp.cumsum(counts) - counts
    dest = pad_start[sorted_e] + jnp.arange(TK, dtype=jnp.int32) - start[sorted_e]
    n_blocks = (TK + N_EXPERTS * (MOE_BLOCK - 1) + MOE_BLOCK - 1) // MOE_BLOCK
    n_slots = n_blocks * MOE_BLOCK
    slot_tok = jnp.zeros((n_slots,), jnp.int32).at[dest].set(flat_tok[order])
    slot_gate = jnp.zeros((n_slots,), jnp.float32).at[dest].set(gates.reshape(-1)[order])
    block_e = jnp.minimum(jnp.searchsorted(pad_end, jnp.arange(n_blocks) * MOE_BLOCK, side='right'), N_EXPERTS - 1)
    xb = h[slot_tok].reshape(n_blocks, MOE_BLOCK, D)

    def expert_block(args):
        xe, e = args
        gu = xe @ w_gu[e] + b_gu[e]
        gate, up = jnp.split(gu, 2, axis=-1)
        gate = jnp.minimum(gate, SWIGLU_LIMIT)
        up = jnp.clip(up, -SWIGLU_LIMIT, SWIGLU_LIMIT)
        act = (up + 1) * gate * jax.nn.sigmoid(SWIGLU_ALPHA * gate)
        return act @ w_dn[e] + b_dn[e]

    yb = lax.map(expert_block, (xb, block_e)).reshape(n_slots, D)
    return jnp.zeros((T, D), h.dtype).at[slot_tok].add(yb * slot_gate[:, None].astype(h.dtype))


def trunk_layer(x, cond, W, l, rope, ctx):
    B, S, D = x.shape
    mods = jnp.split(jax.nn.silu(cond) @ W['ada_w'][l] + W['ada_b'][l], 6, axis=-1)
    sh1, sc1, g1, sh2, sc2, g2 = (m[:, None, :] for m in mods)
    ng = W['norm_g'][l]
    h = rms_norm(x, ng[0]) * (1 + sc1) + sh1
    z = h @ W['w_in'][l]
    o1 = D_RWKV_IN
    o2 = o1 + D_DIFF
    o3 = o2 + D_DIFF
    o4 = o3 + D_DIFF
    o5 = o4 + D_CHUNK
    zr, zq, zk, zv, zu, zg = jnp.split(z, [o1, o2, o3, o4, o5], axis=-1)
    o_r, states = rwkv7_mixer(zr, W, l, None if ctx is None else ctx[2])
    o_d, k, v = diff_mixer(zq, zk, zv, W, l, rope, None if ctx is None else (ctx[0], ctx[1]))
    o_c = chunk_mixer(zu, zg, W, l)
    mix = jnp.concatenate([o_r, o_d, o_c], axis=-1) @ W['w_out'][l]
    x = x + g1 * rms_norm(mix, ng[1])
    h2 = rms_norm(x, ng[2]) * (1 + sc2) + sh2
    f = moe(h2.reshape(B * S, D), W['router_w'][l], W['router_b'][l], W['w_gu'][l], W['b_gu'][l], W['w_dn'][l], W['b_dn'][l]).reshape(B, S, D)
    x = x + g2 * rms_norm(f, ng[3])
    return x, k, v, states


def setup_inputs(seed: int = 0) -> dict:
    key = jax.random.key(seed)
    ks = iter(jax.random.split(key, 40))
    nrm = lambda shape, scale: jax.random.normal(next(ks), shape, jnp.float32) * scale
    uni = lambda shape, lo, hi: jax.random.uniform(next(ks), shape, jnp.float32, lo, hi)
    gain = lambda shape, s=0.02: 1.0 + nrm(shape, s)
    return {
        'x_prompt': nrm((BATCH, SEQ, D_MODEL), 1.0),
        'x_sample': nrm((DEC_BATCH, DEC_SEQ, D_MODEL), 1.0),
        'c': nrm((DEC_BATCH, D_MODEL), 1.0),
        'cache_k': nrm((DEC_BATCH, DEPTH, PAST_LEN, H_DIFF, 2, D_HEAD_DIFF), 1.0),
        'cache_v': nrm((DEC_BATCH, DEPTH, PAST_LEN, H_DIFF, 2 * D_HEAD_DIFF), 1.0),
        'state_wkv': nrm((DEC_BATCH, DEPTH, 2, H_RWKV, N_RWKV, N_RWKV), 0.3),
        'c_ctx': nrm((D_MODEL,), 1.0),
        'ada_w': nrm((DEPTH, D_MODEL, 6 * D_MODEL), 0.5 * D_MODEL ** -0.5),
        'ada_b': nrm((DEPTH, 6 * D_MODEL), 0.02),
        'norm_g': gain((DEPTH, 4, D_MODEL)),
        'w_in': nrm((DEPTH, D_MODEL, D_IN), D_MODEL ** -0.5),
        'shift_mu': uni((DEPTH, 2, D_RWKV_IN), 0.0, 0.5),
        'decay_w0': uni((DEPTH, 2, D_RWKV), -2.0, 1.0),
        'decay_w2': nrm((DEPTH, 2, D_LORA_W, D_RWKV), 0.1 * D_LORA_W ** -0.5),
        'iclr_a0': nrm((DEPTH, 2, D_RWKV), 0.5),
        'iclr_a2': nrm((DEPTH, 2, D_LORA_A, D_RWKV), 0.1 * D_LORA_A ** -0.5),
        'gate_g2': nrm((DEPTH, D_LORA_G, D_RWKV), D_LORA_G ** -0.5),
        'k_k': gain((DEPTH, D_RWKV), 0.1),
        'k_a': gain((DEPTH, D_RWKV), 0.1),
        'r_k': nrm((DEPTH, H_RWKV, N_RWKV), 0.1),
        'ln_x_w': gain((DEPTH, D_RWKV)),
        'ln_x_b': nrm((DEPTH, D_RWKV), 0.02),
        'diff_lam': nrm((DEPTH, 4, D_HEAD_DIFF), 0.1),
        'diff_subln': gain((DEPTH, 2 * D_HEAD_DIFF)),
        'chunk_vnorm': gain((DEPTH, D_CHUNK)),
        'chunk_ws': nrm((DEPTH, G_CHUNK, CHUNK, CHUNK), CHUNK ** -0.5),
        'chunk_bs': nrm((DEPTH, G_CHUNK, CHUNK), 0.02),
        'w_out': nrm((DEPTH, D_MIX, D_MODEL), D_MIX ** -0.5),
        'router_w': nrm((DEPTH, D_MODEL, N_EXPERTS), D_MODEL ** -0.5),
        'router_b': nrm((DEPTH, N_EXPERTS), 0.01),
        'w_gu': nrm((DEPTH, N_EXPERTS, D_MODEL, 2 * D_FF_EXPERT), D_MODEL ** -0.5),
        'b_gu': nrm((DEPTH, N_EXPERTS, 2 * D_FF_EXPERT), 0.02),
        'w_dn': nrm((DEPTH, N_EXPERTS, D_FF_EXPERT, D_MODEL), D_FF_EXPERT ** -0.5),
        'b_dn': nrm((DEPTH, N_EXPERTS, D_MODEL), 0.02),
    }


def reference(x_prompt, x_sample, c, cache_k, cache_v, state_wkv, c_ctx, ada_w, ada_b, norm_g, w_in, shift_mu, decay_w0, decay_w2, iclr_a0, iclr_a2, gate_g2, k_k, k_a, r_k, ln_x_w, ln_x_b, diff_lam, diff_subln, chunk_vnorm, chunk_ws, chunk_bs, w_out, router_w, router_b, w_gu, b_gu, w_dn, b_dn):
    W = dict(ada_w=ada_w, ada_b=ada_b, norm_g=norm_g, w_in=w_in, shift_mu=shift_mu, decay_w0=decay_w0, decay_w2=decay_w2, iclr_a0=iclr_a0, iclr_a2=iclr_a2, gate_g2=gate_g2, k_k=k_k, k_a=k_a, r_k=r_k, ln_x_w=ln_x_w, ln_x_b=ln_x_b, diff_lam=diff_lam, diff_subln=diff_subln, chunk_vnorm=chunk_vnorm, chunk_ws=chunk_ws, chunk_bs=chunk_bs, w_out=w_out, router_w=router_w, router_b=router_b, w_gu=w_gu, b_gu=b_gu, w_dn=w_dn, b_dn=b_dn)
    xp = x_prompt
    ks, vs, ss = [], [], []
    for l in range(DEPTH):
        xp, k_l, v_l, s_l = trunk_layer(xp, c_ctx[None, :], W, l, None, None)
        ks.append(k_l)
        vs.append(v_l)
        ss.append(s_l.astype(xp.dtype))
    new_cache_k = jnp.stack(ks, axis=1)
    new_cache_v = jnp.stack(vs, axis=1)
    new_state_wkv = jnp.stack(ss, axis=1)
    rope = axial_rope(x_sample.shape[1])
    xs = x_sample
    for l in range(DEPTH):
        xs, _, _, _ = trunk_layer(xs, c, W, l, rope, (cache_k[:, l], cache_v[:, l], state_wkv[:, l]))
    return (xp, xs, new_cache_k, new_cache_v, new_state_wkv)
```

```python
import functools
import math

import jax
import jax.numpy as jnp
from jax import lax
from jax.experimental import pallas as pl
from jax.experimental.pallas import tpu as pltpu

F32 = jnp.float32
BF16 = jnp.bfloat16

LANES = 128
HEAD_N = 64
NORM_EPS = 1e-6
GN_EPS = 64e-5
ROPE_THETA = 10000.0
GRID_W = 64
TOP_K = 4
SWIGLU_LIMIT = 7.0
SWIGLU_ALPHA = 1.702
CHUNK = 128
VMEM_LIMIT = 56 * 1024 * 1024


def _cparams(sem):
    return pltpu.CompilerParams(dimension_semantics=sem, vmem_limit_bytes=VMEM_LIMIT)


def _split3(x):
    hi = x.astype(BF16)
    r1 = x - hi.astype(F32)
    mid = r1.astype(BF16)
    lo = (r1 - mid.astype(F32)).astype(BF16)
    return hi, mid, lo


def _segsum(x, bd3):
    outs = []
    for j in range(x.shape[1] // LANES):
        hi, mid, lo = _split3(x[:, j * LANES:(j + 1) * LANES])
        lhs = jnp.concatenate([hi, mid, lo], axis=1)
        outs.append(jnp.dot(lhs, bd3, preferred_element_type=F32))
    return outs[0] if len(outs) == 1 else jnp.concatenate(outs, axis=1)


def _rms(x, g):
    return x * lax.rsqrt(jnp.mean(x * x, axis=-1, keepdims=True) + NORM_EPS) * g


def _ada_kernel(c_ref, w_ref, b_ref, o_ref):
    c = c_ref[...]
    s = c * jax.nn.sigmoid(c)
    o_ref[...] = jnp.dot(s.astype(BF16), w_ref[...].astype(BF16),
                         preferred_element_type=F32) + b_ref[...]


def ada_mods(cond, ada_w, ada_b3, l):
    m, d = cond.shape
    n = ada_w.shape[2]
    tn = 1024
    return pl.pallas_call(
        _ada_kernel,
        grid=(n // tn,),
        in_specs=[pl.BlockSpec((m, d), lambda j: (0, 0)),
                  pl.BlockSpec((None, d, tn), lambda j: (l, 0, j)),
                  pl.BlockSpec((None, 1, tn), lambda j: (l, 0, j))],
        out_specs=pl.BlockSpec((m, tn), lambda j: (0, j)),
        out_shape=jax.ShapeDtypeStruct((m, n), F32),
        compiler_params=_cparams(("arbitrary",)),
        name="ada_mods",
    )(cond, ada_w, ada_b3)


def _inproj_kernel(x_ref, g_ref, sh_ref, sc_ref, w_ref, o_ref, h_ref):
    @pl.when(pl.program_id(1) == 0)
    def _():
        h = _rms(x_ref[...], g_ref[...]) * (1.0 + sc_ref[...]) + sh_ref[...]
        h_ref[...] = h.astype(BF16)

    o_ref[...] = jnp.dot(h_ref[...], w_ref[...], preferred_element_type=F32)


def _mod_row(i, tm, t_ctx, s_dec):
    r0 = i * tm
    return jnp.where(r0 < t_ctx, 0, 1 + (r0 - t_ctx) // s_dec)


def in_proj(x, mods4, norm_g3, w_in_p, l, t_ctx, s_dec):
    t, d = x.shape
    n = w_in_p.shape[2]
    tm, tn = math.gcd(512, t_ctx, s_dec), 512
    mrow = lambda i: _mod_row(i, tm, t_ctx, s_dec)
    return pl.pallas_call(
        _inproj_kernel,
        grid=(t // tm, n // tn),
        in_specs=[pl.BlockSpec((tm, d), lambda i, j: (i, 0)),
                  pl.BlockSpec((None, 1, d), lambda i, j: (4 * l, 0, 0)),
                  pl.BlockSpec((None, None, 1, d), lambda i, j: (mrow(i), 0, 0, 0)),
                  pl.BlockSpec((None, None, 1, d), lambda i, j: (mrow(i), 1, 0, 0)),
                  pl.BlockSpec((None, d, tn), lambda i, j: (l, 0, j))],
        out_specs=pl.BlockSpec((tm, tn), lambda i, j: (i, j)),
        out_shape=jax.ShapeDtypeStruct((t, n), F32),
        scratch_shapes=[pltpu.VMEM((tm, d), BF16)],
        compiler_params=_cparams(("arbitrary", "arbitrary")),
        name="in_proj",
    )(x, norm_g3, mods4, mods4, w_in_p)


def _prep_kernel(z_ref, zp_ref, zn_ref, mu_ref, pv_ref, p2_ref, w2_ref, a2_ref, g2_ref, bd_ref,
                 r_ref, v_ref, kk_ref, w_ref, kd_ref, b_ref, bonus_ref, g_ref, zs_ref,
                 *, tm, t_ctx, s_ctx, s_dec, d_r):
    i = pl.program_id(0)
    r0 = i * tm
    in_ctx = r0 < t_ctx
    pos0 = jnp.where(in_ctx, r0 % s_ctx, (r0 - t_ctx) % s_dec)
    slen = jnp.where(in_ctx, s_ctx, s_dec)
    is_start = pos0 == 0
    is_end = pos0 + tm == slen
    width = z_ref.shape[1]
    rows = lax.broadcasted_iota(jnp.int32, (tm, LANES), 0)
    for j in range(width // LANES):
        sl = slice(j * LANES, (j + 1) * LANES)
        zc = z_ref[:, sl]
        prev_row = jnp.where(is_start, 0.0, zp_ref[7:8, sl])
        next_row = jnp.where(is_end, 0.0, zn_ref[0:1, sl])
        zprev = jnp.where(rows == 0, prev_row, pltpu.roll(zc, 1, axis=0))
        znext = jnp.where(rows == tm - 1, next_row, pltpu.roll(zc, tm - 1, axis=0))
        zs_ref[:, sl] = zc + mu_ref[0:1, sl] * (zprev - zc) + mu_ref[1:2, sl] * (znext - zc)

    bd3 = bd_ref[...]
    r = zs_ref[:, 0:d_r]
    k = zs_ref[:, d_r:2 * d_r]
    v = zs_ref[:, 2 * d_r:3 * d_r]
    o = 3 * d_r
    cw = zs_ref[:, o:o + LANES]
    ca = zs_ref[:, o + LANES:o + 2 * LANES]
    cg = zs_ref[:, o + 2 * LANES:o + 4 * LANES]
    k_k = pv_ref[0:1, :]
    k_a = pv_ref[1:2, :]
    r_k = pv_ref[2:3, :]

    r_ref[...] = r
    v_ref[...] = v
    g_ref[...] = jnp.dot(jax.nn.sigmoid(cg).astype(BF16), g2_ref[...], preferred_element_type=F32)

    kkr = k * k_k
    nrm = jnp.sqrt(_segsum(kkr * kkr, bd3))
    kk = kkr / jnp.maximum(nrm, 1e-12)
    kk_ref[...] = kk

    lw = p2_ref[0:1, :] + jnp.dot(jnp.tanh(cw).astype(BF16), w2_ref[...], preferred_element_type=F32)
    decay = jnp.exp(-math.exp(-0.5) * jax.nn.sigmoid(lw))
    a = jax.nn.sigmoid(p2_ref[1:2, :] + jnp.dot(ca.astype(BF16), a2_ref[...], preferred_element_type=F32))
    kd_sum = None
    for d in range(2):
        a_d = a[:, d * d_r:(d + 1) * d_r]
        kd = k * (1.0 + (a_d - 1.0) * k_a)
        w_ref[d] = decay[:, d * d_r:(d + 1) * d_r]
        kd_ref[d] = kd
        b_ref[d] = kk * a_d
        kd_sum = kd if kd_sum is None else kd_sum + kd
    bonus_ref[...] = _segsum(r * kd_sum * r_k, bd3) * v


def rwkv_prep(z, mu_p, pv, p2, w2bd, a2bd, g2p, bd3, l, t_ctx, s_ctx, s_dec, d_r):
    t = z.shape[0]
    tm = 128
    wz = mu_p.shape[2]
    nb8 = t // 8
    kern = functools.partial(_prep_kernel, tm=tm, t_ctx=t_ctx, s_ctx=s_ctx, s_dec=s_dec, d_r=d_r)
    row = pl.BlockSpec((tm, d_r), lambda i: (i, 0))
    row2 = pl.BlockSpec((2, tm, d_r), lambda i: (0, i, 0))
    o1 = jax.ShapeDtypeStruct((t, d_r), F32)
    o2 = jax.ShapeDtypeStruct((2, t, d_r), F32)
    return pl.pallas_call(
        kern,
        grid=(t // tm,),
        in_specs=[pl.BlockSpec((tm, wz), lambda i: (i, 0)),
                  pl.BlockSpec((8, wz), lambda i: (jnp.maximum(i * (tm // 8) - 1, 0), 0)),
                  pl.BlockSpec((8, wz), lambda i: (jnp.minimum((i + 1) * (tm // 8), nb8 - 1), 0)),
                  pl.BlockSpec((None, 8, wz), lambda i: (l, 0, 0)),
                  pl.BlockSpec((None, 8, d_r), lambda i: (l, 0, 0)),
                  pl.BlockSpec((None, 8, 2 * d_r), lambda i: (l, 0, 0)),
                  pl.BlockSpec((None, LANES, 2 * d_r), lambda i: (l, 0, 0)),
                  pl.BlockSpec((None, LANES, 2 * d_r), lambda i: (l, 0, 0)),
                  pl.BlockSpec((None, 2 * LANES, d_r), lambda i: (l, 0, 0)),
                  pl.BlockSpec((3 * LANES, LANES), lambda i: (0, 0))],
        out_specs=[row, row, row, row2, row2, row2, row, row],
        out_shape=[o1, o1, o1, o2, o2, o2, o1, o1],
        scratch_shapes=[pltpu.VMEM((tm, wz), F32)],
        compiler_params=_cparams(("arbitrary",)),
        name="rwkv_prep",
    )(z, z, z, mu_p, pv, p2, w2bd, a2bd, g2p, bd3)


def _scan_kernel(r_ref, kk_ref, v_ref, w_ref, kd_ref, b_ref, s0_ref, bd_ref,
                 y_ref, sf_ref, s_ref, *, tc, nt, d_r):
    d = pl.program_id(1)
    c = pl.program_id(2)

    @pl.when(c == 0)
    def _():
        s_ref[...] = s0_ref[...]

    imask = (lax.broadcasted_iota(jnp.int32, (HEAD_N, LANES), 0)
             == (lax.broadcasted_iota(jnp.int32, (HEAD_N, LANES), 1) & (HEAD_N - 1)))
    bd3 = bd_ref[...]

    def col_tile(parts, sl):
        lhs = jnp.concatenate(
            [jnp.where(imask, jnp.broadcast_to(p[:, sl].astype(F32), (HEAD_N, LANES)), 0.0).astype(BF16)
             for p in parts], axis=1)
        return jnp.dot(lhs, bd3, preferred_element_type=F32)

    def step(s, carry):
        t = jnp.where(d == 0, s, tc - 1 - s)
        parts = [_split3(ref[pl.ds(t, 1), :]) for ref in (kk_ref, w_ref, kd_ref, b_ref, r_ref)]
        vrow = v_ref[pl.ds(t, 1), :]
        ys = []
        for j in range(d_r // LANES):
            sl = slice(j * LANES, (j + 1) * LANES)
            kkc, wc, kdc, bc, rc = [col_tile(p, sl) for p in parts]
            st = s_ref[:, sl]
            sa = jnp.sum(st * kkc, axis=0, keepdims=True)
            st = st * wc - bc * sa + kdc * vrow[:, sl]
            s_ref[:, sl] = st
            ys.append(jnp.sum(st * rc, axis=0, keepdims=True))
        y_ref[pl.ds(t, 1), :] = jnp.concatenate(ys, axis=1)
        return carry

    lax.fori_loop(0, tc, step, 0)

    @pl.when(c == nt - 1)
    def _():
        sf_ref[...] = s_ref[...]


def wkv_scan(r, kk, v, w, kd, b, s0, bd3, row0, n_seq, s_len, d_r):
    tc = min(256, s_len)
    nt = s_len // tc
    blk0 = row0 // tc

    def tblk(bi, d, c):
        return blk0 + bi * nt + c + d * (nt - 1 - 2 * c)

    shared = pl.BlockSpec((tc, d_r), lambda bi, d, c: (tblk(bi, d, c), 0))
    perdir = pl.BlockSpec((None, tc, d_r), lambda bi, d, c: (d, tblk(bi, d, c), 0))
    state = pl.BlockSpec((None, None, HEAD_N, d_r), lambda bi, d, c: (bi, d, 0, 0))
    kern = functools.partial(_scan_kernel, tc=tc, nt=nt, d_r=d_r)
    return pl.pallas_call(
        kern,
        grid=(n_seq, 2, nt),
        in_specs=[shared, shared, shared, perdir, perdir, perdir, state,
                  pl.BlockSpec((3 * LANES, LANES), lambda bi, d, c: (0, 0))],
        out_specs=[pl.BlockSpec((None, tc, d_r),
                                lambda bi, d, c: (d, bi * nt + c + d * (nt - 1 - 2 * c), 0)),
                   state],
        out_shape=[jax.ShapeDtypeStruct((2, n_seq * s_len, d_r), F32),
                   jax.ShapeDtypeStruct((n_seq, 2, HEAD_N, d_r), F32)],
        scratch_shapes=[pltpu.VMEM((HEAD_N, d_r), F32)],
        compiler_params=_cparams(("arbitrary", "arbitrary", "arbitrary")),
        name="wkv_scan",
    )(r, kk, v, w, kd, b, s0, bd3)


def _post_kernel(y_ref, bonus_ref, g_ref, pv_ref, bd_ref, o_ref):
    bd3 = bd_ref[...]
    y = y_ref[0] + y_ref[1]
    mu = _segsum(y, bd3) * (1.0 / HEAD_N)
    yc = y - mu
    var = _segsum(yc * yc, bd3) * (1.0 / HEAD_N)
    yn = yc * lax.rsqrt(var + GN_EPS) * pv_ref[3:4, :] + pv_ref[4:5, :]
    o_ref[...] = (yn + bonus_ref[...]) * g_ref[...]


def rwkv_post(y, bonus, g, pv, bd3, l):
    t, d_r = bonus.shape
    tm = 256
    row = pl.BlockSpec((tm, d_r), lambda i: (i, 0))
    return pl.pallas_call(
        _post_kernel,
        grid=(t // tm,),
        in_specs=[pl.BlockSpec((2, tm, d_r), lambda i: (0, i, 0)), row, row,
                  pl.BlockSpec((None, 8, d_r), lambda i: (l, 0, 0)),
                  pl.BlockSpec((3 * LANES, LANES), lambda i: (0, 0))],
        out_specs=row,
        out_shape=jax.ShapeDtypeStruct((t, d_r), F32),
        compiler_params=_cparams(("arbitrary",)),
        name="rwkv_post",
    )(y, bonus, g, pv, bd3)


def _rope(x, cos, sin):
    lane = lax.broadcasted_iota(jnp.int32, (x.shape[0], LANES), 1)
    first = (lane & 31) < 16
    outs = []
    for j in range(x.shape[1] // LANES):
        xc = x[:, j * LANES:(j + 1) * LANES]
        partner = jnp.where(first, pltpu.roll(xc, LANES - 16, axis=1), pltpu.roll(xc, 16, axis=1))
        outs.append(xc * cos + partner * sin)
    return outs[0] if len(outs) == 1 else jnp.concatenate(outs, axis=1)


def _attn_kernel(lam_ref, q_ref, k_ref, v_ref, *rest, use_ctx, n_heads, out_scale):
    if use_ctx:
        (ck_ref, cv_ref, cosq_ref, sinq_ref, cosk_ref, sink_ref, g_ref,
         o_ref, kb_ref, vb_ref, ckb_ref, cvb_ref) = rest
    else:
        g_ref, o_ref, kb_ref, vb_ref = rest
    qi = pl.program_id(1)

    @pl.when(qi == 0)
    def _():
        k = k_ref[...]
        if use_ctx:
            k = _rope(k, cosk_ref[...], sink_ref[...])
            ckb_ref[...] = ck_ref[...].astype(BF16)
            cvb_ref[...] = cv_ref[...].astype(BF16)
        kb_ref[...] = k.astype(BF16)
        vb_ref[...] = v_ref[...].astype(BF16)

    lam = lam_ref[0]
    q = q_ref[...]
    if use_ctx:
        q = _rope(q, cosq_ref[...], sinq_ref[...])
    tq = q.shape[0]
    lane = lax.broadcasted_iota(jnp.int32, (tq, LANES), 1)
    scale = HEAD_N ** -0.5
    dn = (((1,), (1,)), ((), ()))
    for h in range(n_heads):
        sl = slice(h * LANES, (h + 1) * LANES)
        qh = q[:, sl]
        kh = kb_ref[:, sl]
        probs = []
        for comp in range(2):
            qc = jnp.where((lane < HEAD_N) == (comp == 0), qh, 0.0).astype(BF16)
            s_a = lax.dot_general(qc, kh, dn, preferred_element_type=F32) * scale
            m = jnp.max(s_a, axis=-1, keepdims=True)
            if use_ctx:
                s_b = lax.dot_general(qc, ckb_ref[:, sl], dn, preferred_element_type=F32) * scale
                m = jnp.maximum(m, jnp.max(s_b, axis=-1, keepdims=True))
            e_a = jnp.exp(s_a - m)
            den = jnp.sum(e_a, axis=-1, keepdims=True)
            if use_ctx:
                e_b = jnp.exp(s_b - m)
                den = den + jnp.sum(e_b, axis=-1, keepdims=True)
                probs.append((e_a / den, e_b / den))
            else:
                probs.append((e_a / den,))
        pd = [p1 - lam * p2 for p1, p2 in zip(*probs)]
        o = jnp.dot(pd[0].astype(BF16), vb_ref[:, sl], preferred_element_type=F32)
        if use_ctx:
            o = o + jnp.dot(pd[1].astype(BF16), cvb_ref[:, sl], preferred_element_type=F32)
        o_ref[:, sl] = _rms(o, g_ref[...]) * out_scale


def diff_attn(z, lam, subln3, l, lam_init, row0, n_seq, s_len, d_diff, col0, ctx=None):
    tq = 128
    nq = s_len // tq
    n_heads = d_diff // LANES
    qb0 = row0 // tq
    sb0 = row0 // s_len
    use_ctx = ctx is not None
    in_specs = [pl.BlockSpec(memory_space=pltpu.SMEM),
                pl.BlockSpec((tq, d_diff), lambda bi, qi: (qb0 + bi * nq + qi, col0)),
                pl.BlockSpec((s_len, d_diff), lambda bi, qi: (sb0 + bi, col0 + 1)),
                pl.BlockSpec((s_len, d_diff), lambda bi, qi: (sb0 + bi, col0 + 2))]
    args = [lam, z, z, z]
    scratch = [pltpu.VMEM((s_len, d_diff), BF16), pltpu.VMEM((s_len, d_diff), BF16)]
    if use_ctx:
        cache_k, cache_v, cos, sin = ctx
        p_len = cache_k.shape[2]
        cspec = pl.BlockSpec((None, None, p_len, d_diff), lambda bi, qi: (bi, l, 0, 0))
        in_specs += [cspec, cspec,
                     pl.BlockSpec((tq, LANES), lambda bi, qi: (qi, 0)),
                     pl.BlockSpec((tq, LANES), lambda bi, qi: (qi, 0)),
                     pl.BlockSpec((s_len, LANES), lambda bi, qi: (0, 0)),
                     pl.BlockSpec((s_len, LANES), lambda bi, qi: (0, 0))]
        args += [cache_k, cache_v, cos, sin, cos, sin]
        scratch += [pltpu.VMEM((p_len, d_diff), BF16), pltpu.VMEM((p_len, d_diff), BF16)]
    in_specs.append(pl.BlockSpec((None, 1, LANES), lambda bi, qi: (l, 0, 0)))
    args.append(subln3)
    kern = functools.partial(_attn_kernel, use_ctx=use_ctx, n_heads=n_heads,
                             out_scale=1.0 - lam_init)
    return pl.pallas_call(
        kern,
        grid=(n_seq, nq),
        in_specs=in_specs,
        out_specs=pl.BlockSpec((tq, d_diff), lambda bi, qi: (bi * nq + qi, 0)),
        out_shape=jax.ShapeDtypeStruct((n_seq * s_len, d_diff), F32),
        scratch_shapes=scratch,
        compiler_params=_cparams(("arbitrary", "arbitrary")),
        name="diff_attn_ctx" if use_ctx else "diff_attn",
    )(*args)


def _chunk_kernel(u_ref, g_ref, vn_ref, ws_ref, bs_ref, o_ref):
    u = jax.nn.gelu(u_ref[...], approximate=True)
    vv = _rms(jax.nn.gelu(g_ref[...], approximate=True), vn_ref[...])
    for gi in range(ws_ref.shape[0]):
        sl = slice(gi * LANES, (gi + 1) * LANES)
        s = jnp.dot(ws_ref[gi].astype(BF16), vv[:, sl].astype(BF16), preferred_element_type=F32)
        o_ref[:, sl] = u[:, sl] * (s + bs_ref[:, sl])


def chunk_mix(z, vn3, chunk_ws, bs_exp, l, d_c, col_u):
    t = z.shape[0]
    n_g = chunk_ws.shape[1]
    return pl.pallas_call(
        _chunk_kernel,
        grid=(t // CHUNK,),
        in_specs=[pl.BlockSpec((CHUNK, d_c), lambda i: (i, col_u)),
                  pl.BlockSpec((CHUNK, d_c), lambda i: (i, col_u + 1)),
                  pl.BlockSpec((None, 1, d_c), lambda i: (l, 0, 0)),
                  pl.BlockSpec((None, n_g, CHUNK, CHUNK), lambda i: (l, 0, 0, 0)),
                  pl.BlockSpec((None, CHUNK, d_c), lambda i: (l, 0, 0))],
        out_specs=pl.BlockSpec((CHUNK, d_c), lambda i: (i, 0)),
        out_shape=jax.ShapeDtypeStruct((t, d_c), F32),
        compiler_params=_cparams(("arbitrary",)),
        name="chunk_mix",
    )(z, z, vn3, chunk_ws, bs_exp)


def _outproj_kernel(x_ref, or_ref, od_ref, oc_ref, wr_ref, wd_ref, wc_ref, g1n_ref, g2n_ref,
                    gate_ref, sh_ref, sc_ref, rw_ref, rb_ref, x1_ref, h2_ref, lg_ref):
    mix = jnp.dot(or_ref[...].astype(BF16), wr_ref[...], preferred_element_type=F32)
    mix = mix + jnp.dot(od_ref[...].astype(BF16), wd_ref[...], preferred_element_type=F32)
    mix = mix + jnp.dot(oc_ref[...].astype(BF16), wc_ref[...], preferred_element_type=F32)
    x1 = x_ref[...] + gate_ref[...] * _rms(mix, g1n_ref[...])
    x1_ref[...] = x1
    h2 = _rms(x1, g2n_ref[...]) * (1.0 + sc_ref[...]) + sh_ref[...]
    h2_ref[...] = h2
    hh, hm, hl = _split3(h2)
    wh, wm, wl = _split3(rw_ref[...])
    acc = jnp.dot(hh, wh, preferred_element_type=F32)
    for a, b in ((hh, wm), (hm, wh), (hm, wm), (hh, wl), (hl, wh)):
        acc = acc + jnp.dot(a, b, preferred_element_type=F32)
    lg_ref[...] = acc + rb_ref[...]


def out_proj(x, o_r, o_d, o_c, w_out_b, norm_g3, mods4, rw_p, rb_p, l, t_ctx, s_dec):
    t, d = x.shape
    d_r, d_d, d_c = o_r.shape[1], o_d.shape[1], o_c.shape[1]
    tm = math.gcd(256, t_ctx, s_dec)
    mrow = lambda i: _mod_row(i, tm, t_ctx, s_dec)
    nrm = lambda k: pl.BlockSpec((None, 1, d), lambda i: (4 * l + k, 0, 0))
    mod = lambda k: pl.BlockSpec((None, None, 1, d), lambda i: (mrow(i), k, 0, 0))
    row = lambda w: pl.BlockSpec((tm, w), lambda i: (i, 0))
    nrb = d_r // d_d
    return pl.pallas_call(
        _outproj_kernel,
        grid=(t // tm,),
        in_specs=[row(d), row(d_r), row(d_d), row(d_c),
                  pl.BlockSpec((None, d_r, d), lambda i: (l, 0, 0)),
                  pl.BlockSpec((None, d_d, d), lambda i: (l, nrb, 0)),
                  pl.BlockSpec((None, d_c, d), lambda i: (l, nrb + 1, 0)),
                  nrm(1), nrm(2), mod(2), mod(3), mod(4),
                  pl.BlockSpec((None, d, LANES), lambda i: (l, 0, 0)),
                  pl.BlockSpec((None, 1, LANES), lambda i: (l, 0, 0))],
        out_specs=[row(d), row(d), row(LANES)],
        out_shape=[jax.ShapeDtypeStruct((t, d), F32), jax.ShapeDtypeStruct((t, d), F32),
                   jax.ShapeDtypeStruct((t, LANES), F32)],
        compiler_params=_cparams(("arbitrary",)),
        name="out_proj",
    )(x, o_r, o_d, o_c, w_out_b, w_out_b, w_out_b, norm_g3, norm_g3, mods4, mods4, mods4, rw_p, rb_p)


def _gather_kernel(idx_ref, src_ref, o_ref, sem, *, bm):
    def row_copy(r, src_row):
        return pltpu.make_async_copy(src_ref.at[src_row], o_ref.at[r], sem)

    def issue(r, carry):
        row_copy(r, idx_ref[0, r]).start()
        return carry

    def drain(r, carry):
        row_copy(r, 0).wait()
        return carry

    lax.fori_loop(0, bm, issue, 0)
    lax.fori_loop(0, bm, drain, 0)


def gather_rows(src, idx, bm=128):
    n, d = src.shape
    m = idx.shape[0]
    out = pl.pallas_call(
        functools.partial(_gather_kernel, bm=bm),
        grid=(m // bm,),
        in_specs=[pl.BlockSpec((None, 1, bm), lambda i: (i, 0, 0), memory_space=pltpu.SMEM),
                  pl.BlockSpec(memory_space=pl.ANY)],
        out_specs=pl.BlockSpec((bm, 1, d), lambda i: (i, 0, 0)),
        out_shape=jax.ShapeDtypeStruct((m, 1, d), src.dtype),
        scratch_shapes=[pltpu.SemaphoreType.DMA],
        compiler_params=_cparams(("arbitrary",)),
        name="gather_rows",
    )(idx.reshape(m // bm, 1, bm), src.reshape(n, 1, d))
    return out.reshape(m, d)


def _expert_kernel(be_ref, bv_ref, x_ref, wg_ref, wu_ref, bg_ref, bu_ref, wd_ref, bdn_ref,
                   gate_ref, o_ref, xb_ref, *, nf):
    i = pl.program_id(0)
    f = pl.program_id(1)

    @pl.when(f == 0)
    def _():
        xb_ref[...] = x_ref[...].astype(BF16)
        o_ref[...] = jnp.zeros_like(o_ref)

    @pl.when(bv_ref[i] > 0)
    def _():
        xb = xb_ref[...]
        gt = jnp.dot(xb, wg_ref[...], preferred_element_type=F32) + bg_ref[...]
        up = jnp.dot(xb, wu_ref[...], preferred_element_type=F32) + bu_ref[...]
        gt = jnp.minimum(gt, SWIGLU_LIMIT)
        up = jnp.clip(up, -SWIGLU_LIMIT, SWIGLU_LIMIT)
        act = (up + 1.0) * gt * jax.nn.sigmoid(SWIGLU_ALPHA * gt)
        o_ref[...] += jnp.dot(act.astype(BF16), wd_ref[...], preferred_element_type=F32)

    @pl.when(f == nf - 1)
    def _():
        o_ref[...] = (o_ref[...] + bdn_ref[...]) * gate_ref[...]


def expert_ffn(xs, slot_gate, block_e, block_valid, w_gu_b, b_gu3, w_dn_b, b_dn3, l, n_e, bm):
    n_slots, d = xs.shape
    ff = w_dn_b.shape[1]
    ft = 512
    nf = ff // ft
    nb = n_slots // bm
    e0 = l * n_e

    def fsel(i, f, bv):
        return jnp.where(bv[i] > 0, f, nf - 1)

    return pl.pallas_call(
        functools.partial(_expert_kernel, nf=nf),
        grid_spec=pltpu.PrefetchScalarGridSpec(
            num_scalar_prefetch=2,
            grid=(nb, nf),
            in_specs=[pl.BlockSpec((bm, d), lambda i, f, be, bv: (i, 0)),
                      pl.BlockSpec((None, d, ft), lambda i, f, be, bv: (e0 + be[i], 0, fsel(i, f, bv))),
                      pl.BlockSpec((None, d, ft), lambda i, f, be, bv: (e0 + be[i], 0, nf + fsel(i, f, bv))),
                      pl.BlockSpec((None, 1, ft), lambda i, f, be, bv: (e0 + be[i], 0, fsel(i, f, bv))),
                      pl.BlockSpec((None, 1, ft), lambda i, f, be, bv: (e0 + be[i], 0, nf + fsel(i, f, bv))),
                      pl.BlockSpec((None, ft, d), lambda i, f, be, bv: (e0 + be[i], fsel(i, f, bv), 0)),
                      pl.BlockSpec((None, 1, d), lambda i, f, be, bv: (e0 + be[i], 0, 0)),
                      pl.BlockSpec((bm, 1), lambda i, f, be, bv: (i, 0))],
            out_specs=pl.BlockSpec((bm, d), lambda i, f, be, bv: (i, 0)),
            scratch_shapes=[pltpu.VMEM((bm, d), BF16)]),
        out_shape=jax.ShapeDtypeStruct((n_slots, d), F32),
        compiler_params=_cparams(("arbitrary", "arbitrary")),
        name="expert_ffn",
    )(block_e, block_valid, xs, w_gu_b, w_gu_b, b_gu3, b_gu3, w_dn_b, b_dn3, slot_gate)


def _combine_kernel(x_ref, y_ref, gn_ref, gate_ref, o_ref, *, d):
    f = y_ref[:, 0:d]
    for k in range(1, TOP_K):
        f = f + y_ref[:, k * d:(k + 1) * d]
    o_ref[...] = x_ref[...] + gate_ref[...] * _rms(f, gn_ref[...])


def moe_combine(x1, yk, norm_g3, mods4, l, t_ctx, s_dec):
    t, d = x1.shape
    tm = 128
    mrow = lambda i: _mod_row(i, tm, t_ctx, s_dec)
    return pl.pallas_call(
        functools.partial(_combine_kernel, d=d),
        grid=(t // tm,),
        in_specs=[pl.BlockSpec((tm, d), lambda i: (i, 0)),
                  pl.BlockSpec((tm, TOP_K * d), lambda i: (i, 0)),
                  pl.BlockSpec((None, 1, d), lambda i: (4 * l + 3, 0, 0)),
                  pl.BlockSpec((None, None, 1, d), lambda i: (mrow(i), 5, 0, 0))],
        out_specs=pl.BlockSpec((tm, d), lambda i: (i, 0)),
        out_shape=jax.ShapeDtypeStruct((t, d), F32),
        compiler_params=_cparams(("arbitrary",)),
        name="moe_combine",
    )(x1, yk, norm_g3, mods4)


def _route(logits, n_e, bm):
    t = logits.shape[0]
    top_v, top_i = lax.top_k(logits, TOP_K)
    gates = jax.nn.softmax(top_v, axis=-1)
    tk = t * TOP_K
    flat_e = top_i.reshape(-1)
    onehot = (flat_e[:, None] == jnp.arange(n_e, dtype=jnp.int32)[None, :]).astype(jnp.int32)
    csum = jnp.cumsum(onehot, axis=0)
    counts = csum[-1]
    rank = jnp.sum((csum - onehot) * onehot, axis=1)
    padded = ((counts + bm - 1) // bm) * bm
    pad_end = jnp.cumsum(padded)
    pad_start = pad_end - padded
    pos = pad_start[flat_e] + rank
    nb = (tk + n_e * (bm - 1) + bm - 1) // bm
    n_slots = nb * bm
    flat_tok = jnp.arange(tk, dtype=jnp.int32) // TOP_K
    slot_tok = jnp.zeros((n_slots,), jnp.int32).at[pos].set(flat_tok)
    slot_gate = jnp.zeros((n_slots,), F32).at[pos].set(gates.reshape(-1))
    starts = jnp.arange(nb, dtype=jnp.int32) * bm
    block_e = jnp.minimum(jnp.searchsorted(pad_end, starts, side='right'), n_e - 1).astype(jnp.int32)
    block_valid = (starts < pad_end[-1]).astype(jnp.int32)
    last_e = jnp.max(jnp.where(counts > 0, jnp.arange(n_e, dtype=jnp.int32), 0))
    block_e = jnp.where(block_valid > 0, block_e, last_e)
    return slot_tok, slot_gate.reshape(n_slots, 1), block_e, block_valid, pos.astype(jnp.int32)


def _rope_tables(s_len):
    rows = s_len // GRID_W
    row = jnp.repeat(jnp.arange(rows), GRID_W)
    col = jnp.tile(jnp.arange(GRID_W), rows)
    half = HEAD_N // 2
    inv_freq = 1.0 / (ROPE_THETA ** (jnp.arange(0, half, 2, dtype=F32) / half))
    ang = jnp.stack([row, col], axis=-1).astype(F32)[..., None] * inv_freq
    cos, sin = jnp.cos(ang), jnp.sin(ang)
    cos64 = jnp.concatenate([cos[:, 0], cos[:, 0], cos[:, 1], cos[:, 1]], axis=-1)
    sin64 = jnp.concatenate([-sin[:, 0], sin[:, 0], -sin[:, 1], sin[:, 1]], axis=-1)
    return jnp.tile(cos64, (1, 2)), jnp.tile(sin64, (1, 2))


def kernel(x_prompt, x_sample, c, cache_k, cache_v, state_wkv, c_ctx, ada_w, ada_b, norm_g, w_in, shift_mu, decay_w0, decay_w2, iclr_a0, iclr_a2, gate_g2, k_k, k_a, r_k, ln_x_w, ln_x_b, diff_lam, diff_subln, chunk_vnorm, chunk_ws, chunk_bs, w_out, router_w, router_b, w_gu, b_gu, w_dn, b_dn):
    b_ctx, s_ctx, d = x_prompt.shape
    b_dec, s_dec, _ = x_sample.shape
    depth = ada_w.shape[0]
    t_ctx, t_dec = b_ctx * s_ctx, b_dec * s_dec
    d_r = k_k.shape[1]
    n_h = d_r // HEAD_N
    d_rin = shift_mu.shape[2]
    d_diff = diff_subln.shape[1] * (cache_k.shape[3])
    n_g = chunk_ws.shape[1]
    d_c = n_g * chunk_ws.shape[2]
    n_e = router_w.shape[2]
    d_lw, d_la, d_lg = decay_w2.shape[2], iclr_a2.shape[2], gate_g2.shape[1]
    p_len = cache_k.shape[2]
    assert d_lw == HEAD_N and d_la == HEAD_N and d_lg <= 2 * LANES and d_diff == d_c
    wz = 3 * d_r + 4 * LANES
    col_q = wz // d_diff
    assert wz % d_diff == 0 and d_rin <= wz

    padc = wz - d_rin
    w_in_p = jnp.concatenate([w_in[:, :, :d_rin], jnp.zeros((depth, d, padc), F32), w_in[:, :, d_rin:]],
                             axis=2).astype(BF16)
    mu_p = jnp.pad(shift_mu, ((0, 0), (0, 6), (0, padc)))
    pv = jnp.stack([k_k, k_a, r_k.reshape(depth, d_r), ln_x_w, ln_x_b], axis=1)
    pv = jnp.pad(pv, ((0, 0), (0, 3), (0, 0)))
    p2 = jnp.stack([decay_w0.reshape(depth, 2 * d_r), iclr_a0.reshape(depth, 2 * d_r)], axis=1)
    p2 = jnp.pad(p2, ((0, 0), (0, 6), (0, 0)))
    zl = jnp.zeros((depth, HEAD_N, d_r), F32)
    w2bd = jnp.concatenate([jnp.concatenate([decay_w2[:, 0], zl], axis=2),
                            jnp.concatenate([zl, decay_w2[:, 1]], axis=2)], axis=1).astype(BF16)
    a2bd = jnp.concatenate([jnp.concatenate([iclr_a2[:, 0], zl], axis=2),
                            jnp.concatenate([zl, iclr_a2[:, 1]], axis=2)], axis=1).astype(BF16)
    g2p = jnp.pad(gate_g2, ((0, 0), (0, 2 * LANES - d_lg), (0, 0))).astype(BF16)
    lane = jnp.arange(LANES)
    bd = (lane[:, None] // HEAD_N == lane[None, :] // HEAD_N).astype(BF16)
    bd3 = jnp.concatenate([bd, bd, bd], axis=0)
    norm_g3 = norm_g.reshape(depth * 4, 1, d)
    ada_b3 = ada_b.reshape(depth, 1, 6 * d)
    subln3 = diff_subln.reshape(depth, 1, LANES)
    vn3 = chunk_vnorm.reshape(depth, 1, d_c)
    bs_exp = jnp.repeat(jnp.swapaxes(chunk_bs, 1, 2), d_c // n_g, axis=2)
    w_out_b = w_out.astype(BF16)
    rw_p = jnp.pad(router_w, ((0, 0), (0, 0), (0, LANES - n_e)))
    rb_p = jnp.pad(router_b, ((0, 0), (0, LANES - n_e))).reshape(depth, 1, LANES)
    w_gu_b = w_gu.astype(BF16).reshape(depth * n_e, d, -1)
    w_dn_b = w_dn.astype(BF16).reshape(depth * n_e, -1, d)
    b_gu3 = b_gu.reshape(depth * n_e, 1, -1)
    b_dn3 = b_dn.reshape(depth * n_e, 1, d)
    cos_t, sin_t = _rope_tables(s_dec)
    ck4 = cache_k.reshape(b_dec, depth, p_len, d_diff)
    cv4 = cache_v.reshape(b_dec, depth, p_len, d_diff)
    s0_dec = jnp.transpose(state_wkv, (0, 1, 2, 5, 3, 4)).reshape(b_dec, depth, 2, HEAD_N, d_r)
    s0_ctx = jnp.zeros((b_ctx, 2, HEAD_N, d_r), F32)

    cond = jnp.concatenate([c_ctx[None, :], c], axis=0)
    m_rows = -(-cond.shape[0] // 8) * 8
    cond = jnp.pad(cond, ((0, m_rows - cond.shape[0]), (0, 0)))

    x = jnp.concatenate([x_prompt.reshape(t_ctx, d), x_sample.reshape(t_dec, d)], axis=0)
    bm = 512
    ks, vs, ss = [], [], []
    for l in range(depth):
        mods4 = ada_mods(cond, ada_w, ada_b3, l).reshape(m_rows, 6, 1, d)
        z = in_proj(x, mods4, norm_g3, w_in_p, l, t_ctx, s_dec)
        ks.append(z[:t_ctx, (col_q + 1) * d_diff:(col_q + 2) * d_diff])
        vs.append(z[:t_ctx, (col_q + 2) * d_diff:(col_q + 3) * d_diff])

        r, v, kk, w, kd, b, bonus, g = rwkv_prep(z, mu_p, pv, p2, w2bd, a2bd, g2p, bd3, l,
                                                 t_ctx, s_ctx, s_dec, d_r)
        y_c, sf_c = wkv_scan(r, kk, v, w, kd, b, s0_ctx, bd3, 0, b_ctx, s_ctx, d_r)
        y_d, _ = wkv_scan(r, kk, v, w, kd, b, s0_dec[:, l], bd3, t_ctx, b_dec, s_dec, d_r)
        ss.append(sf_c)
        o_r = rwkv_post(jnp.concatenate([y_c, y_d], axis=1), bonus, g, pv, bd3, l)

        lam_init = 0.8 - 0.6 * math.exp(-0.3 * l)
        lq1, lk1, lq2, lk2 = diff_lam[l].astype(F32)
        lam = (jnp.exp(jnp.sum(lq1 * lk1)) - jnp.exp(jnp.sum(lq2 * lk2)) + lam_init).reshape(1)
        od_c = diff_attn(z, lam, subln3, l, lam_init, 0, b_ctx, s_ctx, d_diff, col_q)
        od_d = diff_attn(z, lam, subln3, l, lam_init, t_ctx, b_dec, s_dec, d_diff, col_q,
                         ctx=(ck4, cv4, cos_t, sin_t))
        o_d = jnp.concatenate([od_c, od_d], axis=0)
        o_c = chunk_mix(z, vn3, chunk_ws, bs_exp, l, d_c, col_q + 3)

        x1, h2, logits = out_proj(x, o_r, o_d, o_c, w_out_b, norm_g3, mods4, rw_p, rb_p, l, t_ctx, s_dec)
        slot_tok, slot_gate, block_e, block_valid, pos = _route(logits[:, :n_e], n_e, bm)
        xs = gather_rows(h2, slot_tok)
        ys = expert_ffn(xs, slot_gate, block_e, block_valid, w_gu_b, b_gu3, w_dn_b, b_dn3, l, n_e, bm)
        yk = gather_rows(ys, pos).reshape(t_ctx + t_dec, TOP_K * d)
        x = moe_combine(x1, yk, norm_g3, mods4, l, t_ctx, s_dec)

    n_hd = cache_k.shape[3]
    new_k = jnp.stack(ks, axis=1).reshape(b_ctx, s_ctx, depth, n_hd, 2, HEAD_N)
    new_k = jnp.transpose(new_k, (0, 2, 1, 3, 4, 5))
    new_v = jnp.stack(vs, axis=1).reshape(b_ctx, s_ctx, depth, n_hd, 2 * HEAD_N)
    new_v = jnp.transpose(new_v, (0, 2, 1, 3, 4))
    st = jnp.stack(ss, axis=1).reshape(b_ctx, depth, 2, HEAD_N, n_h, HEAD_N)
    new_s = jnp.transpose(st, (0, 1, 2, 4, 5, 3))
    y_prompt = x[:t_ctx].reshape(b_ctx, s_ctx, d)
    y_sample = x[t_ctx:].reshape(b_dec, s_dec, d)
    return (y_prompt, y_sample, new_k, new_v, new_s)
```

```python
import functools
import math

import jax
import jax.numpy as jnp
from jax import lax
from jax.experimental import pallas as pl
from jax.experimental.pallas import tpu as pltpu

F32 = jnp.float32
BF16 = jnp.bfloat16

LANES = 128
HEAD_N = 64
NORM_EPS = 1e-6
GN_EPS = 64e-5
ROPE_THETA = 10000.0
GRID_W = 64
TOP_K = 4
SWIGLU_LIMIT = 7.0
SWIGLU_ALPHA = 1.702
CHUNK = 128
VMEM_LIMIT = 56 * 1024 * 1024


def _cparams(sem):
    return pltpu.CompilerParams(dimension_semantics=sem, vmem_limit_bytes=VMEM_LIMIT)


def _split3(x):
    hi = x.astype(BF16)
    r1 = x - hi.astype(F32)
    mid = r1.astype(BF16)
    lo = (r1 - mid.astype(F32)).astype(BF16)
    return hi, mid, lo


def _segsum(x, bd3):
    outs = []
    for j in range(x.shape[1] // LANES):
        hi, mid, lo = _split3(x[:, j * LANES:(j + 1) * LANES])
        lhs = jnp.concatenate([hi, mid, lo], axis=1)
        outs.append(jnp.dot(lhs, bd3, preferred_element_type=F32))
    return outs[0] if len(outs) == 1 else jnp.concatenate(outs, axis=1)


def _rms(x, g):
    return x * lax.rsqrt(jnp.mean(x * x, axis=-1, keepdims=True) + NORM_EPS) * g


def _ada_kernel(c_ref, w_ref, b_ref, o_ref):
    c = c_ref[...]
    s = c * jax.nn.sigmoid(c)
    o_ref[...] = jnp.dot(s.astype(BF16), w_ref[...].astype(BF16),
                         preferred_element_type=F32) + b_ref[...]


def ada_mods(cond, ada_w, ada_b3, l):
    m, d = cond.shape
    n = ada_w.shape[2]
    tn = 1024
    return pl.pallas_call(
        _ada_kernel,
        grid=(n // tn,),
        in_specs=[pl.BlockSpec((m, d), lambda j: (0, 0)),
                  pl.BlockSpec((None, d, tn), lambda j: (l, 0, j)),
                  pl.BlockSpec((None, 1, tn), lambda j: (l, 0, j))],
        out_specs=pl.BlockSpec((m, tn), lambda j: (0, j)),
        out_shape=jax.ShapeDtypeStruct((m, n), F32),
        compiler_params=_cparams(("arbitrary",)),
        name="ada_mods",
    )(cond, ada_w, ada_b3)


def _inproj_kernel(x_ref, g_ref, sh_ref, sc_ref, w_ref, o_ref, h_ref):
    @pl.when(pl.program_id(1) == 0)
    def _():
        h = _rms(x_ref[...], g_ref[...]) * (1.0 + sc_ref[...]) + sh_ref[...]
        h_ref[...] = h.astype(BF16)

    o_ref[...] = jnp.dot(h_ref[...], w_ref[...], preferred_element_type=F32)


def _mod_row(i, tm, t_ctx, s_dec):
    r0 = i * tm
    return jnp.where(r0 < t_ctx, 0, 1 + (r0 - t_ctx) // s_dec)


def in_proj(x, mods4, norm_g3, w_in_p, l, t_ctx, s_dec):
    t, d = x.shape
    n = w_in_p.shape[2]
    tm, tn = math.gcd(512, t_ctx, s_dec), 512
    mrow = lambda i: _mod_row(i, tm, t_ctx, s_dec)
    return pl.pallas_call(
        _inproj_kernel,
        grid=(t // tm, n // tn),
        in_specs=[pl.BlockSpec((tm, d), lambda i, j: (i, 0)),
                  pl.BlockSpec((None, 1, d), lambda i, j: (4 * l, 0, 0)),
                  pl.BlockSpec((None, None, 1, d), lambda i, j: (mrow(i), 0, 0, 0)),
                  pl.BlockSpec((None, None, 1, d), lambda i, j: (mrow(i), 1, 0, 0)),
                  pl.BlockSpec((None, d, tn), lambda i, j: (l, 0, j))],
        out_specs=pl.BlockSpec((tm, tn), lambda i, j: (i, j)),
        out_shape=jax.ShapeDtypeStruct((t, n), F32),
        scratch_shapes=[pltpu.VMEM((tm, d), BF16)],
        compiler_params=_cparams(("arbitrary", "arbitrary")),
        name="in_proj",
    )(x, norm_g3, mods4, mods4, w_in_p)


def _prep_kernel(z_ref, zp_ref, zn_ref, mu_ref, pv_ref, p2_ref, w2_ref, a2_ref, g2_ref, bd_ref,
                 r_ref, v_ref, kk_ref, w_ref, kd_ref, b_ref, bonus_ref, g_ref, zs_ref,
                 *, tm, t_ctx, s_ctx, s_dec, d_r):
    i = pl.program_id(0)
    r0 = i * tm
    in_ctx = r0 < t_ctx
    pos0 = jnp.where(in_ctx, r0 % s_ctx, (r0 - t_ctx) % s_dec)
    slen = jnp.where(in_ctx, s_ctx, s_dec)
    is_start = pos0 == 0
    is_end = pos0 + tm == slen
    width = z_ref.shape[1]
    rows = lax.broadcasted_iota(jnp.int32, (tm, LANES), 0)
    for j in range(width // LANES):
        sl = slice(j * LANES, (j + 1) * LANES)
        zc = z_ref[:, sl]
        prev_row = jnp.where(is_start, 0.0, zp_ref[7:8, sl])
        next_row = jnp.where(is_end, 0.0, zn_ref[0:1, sl])
        zprev = jnp.where(rows == 0, prev_row, pltpu.roll(zc, 1, axis=0))
        znext = jnp.where(rows == tm - 1, next_row, pltpu.roll(zc, tm - 1, axis=0))
        zs_ref[:, sl] = zc + mu_ref[0:1, sl] * (zprev - zc) + mu_ref[1:2, sl] * (znext - zc)

    bd3 = bd_ref[...]
    r = zs_ref[:, 0:d_r]
    k = zs_ref[:, d_r:2 * d_r]
    v = zs_ref[:, 2 * d_r:3 * d_r]
    o = 3 * d_r
    cw = zs_ref[:, o:o + LANES]
    ca = zs_ref[:, o + LANES:o + 2 * LANES]
    cg = zs_ref[:, o + 2 * LANES:o + 4 * LANES]
    k_k = pv_ref[0:1, :]
    k_a = pv_ref[1:2, :]
    r_k = pv_ref[2:3, :]

    r_ref[...] = r
    v_ref[...] = v
    g_ref[...] = jnp.dot(jax.nn.sigmoid(cg).astype(BF16), g2_ref[...], preferred_element_type=F32)

    kkr = k * k_k
    nrm = jnp.sqrt(_segsum(kkr * kkr, bd3))
    kk = kkr / jnp.maximum(nrm, 1e-12)
    kk_ref[...] = kk

    lw = p2_ref[0:1, :] + jnp.dot(jnp.tanh(cw).astype(BF16), w2_ref[...], preferred_element_type=F32)
    decay = -math.exp(-0.5) * jax.nn.sigmoid(lw)
    a = jax.nn.sigmoid(p2_ref[1:2, :] + jnp.dot(ca.astype(BF16), a2_ref[...], preferred_element_type=F32))
    kd_sum = None
    for d in range(2):
        a_d = a[:, d * d_r:(d + 1) * d_r]
        kd = k * (1.0 + (a_d - 1.0) * k_a)
        w_ref[d] = decay[:, d * d_r:(d + 1) * d_r]
        kd_ref[d] = kd
        b_ref[d] = kk * a_d
        kd_sum = kd if kd_sum is None else kd_sum + kd
    bonus_ref[...] = _segsum(r * kd_sum * r_k, bd3) * v


def rwkv_prep(z, mu_p, pv, p2, w2bd, a2bd, g2p, bd3, l, t_ctx, s_ctx, s_dec, d_r):
    t = z.shape[0]
    tm = 128
    wz = mu_p.shape[2]
    nb8 = t // 8
    kern = functools.partial(_prep_kernel, tm=tm, t_ctx=t_ctx, s_ctx=s_ctx, s_dec=s_dec, d_r=d_r)
    row = pl.BlockSpec((tm, d_r), lambda i: (i, 0))
    row2 = pl.BlockSpec((2, tm, d_r), lambda i: (0, i, 0))
    o1 = jax.ShapeDtypeStruct((t, d_r), F32)
    o2 = jax.ShapeDtypeStruct((2, t, d_r), F32)
    return pl.pallas_call(
        kern,
        grid=(t // tm,),
        in_specs=[pl.BlockSpec((tm, wz), lambda i: (i, 0)),
                  pl.BlockSpec((8, wz), lambda i: (jnp.maximum(i * (tm // 8) - 1, 0), 0)),
                  pl.BlockSpec((8, wz), lambda i: (jnp.minimum((i + 1) * (tm // 8), nb8 - 1), 0)),
                  pl.BlockSpec((None, 8, wz), lambda i: (l, 0, 0)),
                  pl.BlockSpec((None, 8, d_r), lambda i: (l, 0, 0)),
                  pl.BlockSpec((None, 8, 2 * d_r), lambda i: (l, 0, 0)),
                  pl.BlockSpec((None, LANES, 2 * d_r), lambda i: (l, 0, 0)),
                  pl.BlockSpec((None, LANES, 2 * d_r), lambda i: (l, 0, 0)),
                  pl.BlockSpec((None, 2 * LANES, d_r), lambda i: (l, 0, 0)),
                  pl.BlockSpec((3 * LANES, LANES), lambda i: (0, 0))],
        out_specs=[row, row, row, row2, row2, row2, row, row],
        out_shape=[o1, o1, o1, o2, o2, o2, o1, o1],
        scratch_shapes=[pltpu.VMEM((tm, wz), F32)],
        compiler_params=_cparams(("arbitrary",)),
        name="rwkv_prep",
    )(z, z, z, mu_p, pv, p2, w2bd, a2bd, g2p, bd3)


def _split2(x):
    hi = x.astype(BF16)
    lo = (x - hi.astype(F32)).astype(BF16)
    return hi, lo


def _mm3(a, b):
    ah, al = _split2(a)
    bh, bl = _split2(b)
    return jnp.dot(jnp.concatenate([ah, ah, al], axis=1), jnp.concatenate([bh, bl, bh], axis=0),
                   preferred_element_type=F32)


def _mm3_nt(a, b):
    ah, al = _split2(a)
    bh, bl = _split2(b)
    return lax.dot_general(jnp.concatenate([ah, ah, al], axis=1), jnp.concatenate([bh, bl, bh], axis=1),
                           (((1,), (1,)), ((), ())), preferred_element_type=F32)


def _mm3_tn(a, b):
    ah, al = _split2(a)
    bh, bl = _split2(b)
    return lax.dot_general(jnp.concatenate([ah, ah, al], axis=0), jnp.concatenate([bh, bl, bh], axis=0),
                           (((0,), (0,)), ((), ())), preferred_element_type=F32)


def _mm1(a, b):
    return jnp.dot(a.astype(BF16), b.astype(BF16), preferred_element_type=F32)


SCAN_C = 64


def _scan_kernel(r_ref, kk_ref, v_ref, lw_ref, kd_ref, b_ref, s0_ref,
                 y_ref, sf_ref, s_ref, kr_s, t_s, g4_s, gv_s, kb_s, gc_s, *, tb, nt, nlt):
    c_len = SCAN_C
    fwd = pl.program_id(1) == 0
    tstep = pl.program_id(3)
    lane = lax.broadcasted_iota(jnp.int32, (c_len, LANES), 1)
    row = lax.broadcasted_iota(jnp.int32, (c_len, LANES), 0)
    colm = lane & (HEAD_N - 1)
    ahead = (row - colm) * jnp.where(fwd, 1, -1)
    strict = ahead > 0
    incl = ahead >= 0
    h0 = lane < HEAD_N
    lane2 = lax.broadcasted_iota(jnp.int32, (2 * c_len, LANES), 1)
    row2 = lax.broadcasted_iota(jnp.int32, (2 * c_len, LANES), 0)
    bdmask = (lane2 < HEAD_N) == (row2 < HEAD_N)
    eye = lane2 == row2
    m_incl = jnp.where(jnp.logical_and(incl, h0), 1.0, 0.0).astype(BF16)
    m_incl3 = jnp.concatenate([m_incl, m_incl, m_incl], axis=1)
    zpad = jnp.zeros((c_len, LANES), BF16)

    def two_heads(x):
        return jnp.concatenate([jnp.where(h0, x, 0.0), jnp.where(h0, 0.0, x)], axis=0)

    @pl.when(tstep == 0)
    def _():
        for j in range(nlt):
            s_ref[j] = two_heads(s0_ref[:, j * LANES:(j + 1) * LANES])

    n_ch = tb // c_len
    ones_bd = jnp.where(bdmask, 1.0, 0.0).astype(BF16)

    def chunk_rows(ci):
        cc = jnp.where(fwd, ci, n_ch - 1 - ci)
        return pl.ds(pl.multiple_of(cc * c_len, c_len), c_len)

    tiles = range(nlt)
    lanes_of = [slice(j * LANES, (j + 1) * LANES) for j in tiles]
    ones_bd3 = jnp.concatenate([ones_bd, ones_bd, ones_bd], axis=0)
    eye_f = jnp.where(eye, 1.0, 0.0)
    for ci in range(n_ch):
        rows = chunk_rows(ci)
        lws = [lw_ref[rows, sl] for sl in lanes_of]
        lgs = []
        for lw in lws:
            hi, mid, lo = _split3(lw)
            lgs.append(jnp.dot(m_incl3, jnp.concatenate([hi, zpad, mid, zpad, lo, zpad], axis=0),
                               preferred_element_type=F32))
        lasts = [jnp.where(fwd, lg[c_len - 1:c_len, :], lg[0:1, :]) for lg in lgs]
        krs = [jnp.concatenate([kk_ref[rows, sl] * jnp.exp(lg - lw), r_ref[rows, sl] * jnp.exp(lg)], axis=0)
               for sl, lg, lw in zip(lanes_of, lgs, lws)]
        ens = [jnp.exp(-lg) for lg in lgs]
        gas = [_mm3_nt(kr, two_heads(kd_ref[rows, sl] * en)) for kr, sl, en in zip(krs, lanes_of, ens)]
        gbs = [_mm3_nt(kr, two_heads(b_ref[rows, sl] * en)) for kr, sl, en in zip(krs, lanes_of, ens)]
        ps = [-two_heads(jnp.where(strict, gb[:c_len], 0.0)) for gb in gbs]
        accs = [eye_f + p for p in ps]
        span = 2
        while span < c_len:
            ps = [_mm1(p, p) for p in ps]
            accs = [acc + _mm1(acc, p) for acc, p in zip(accs, ps)]
            span *= 2
        gvs = [_mm3(jnp.concatenate([jnp.where(strict, ga[:c_len], 0.0), jnp.where(incl, ga[c_len:], 0.0)],
                                    axis=0), two_heads(v_ref[rows, sl]))
               for ga, sl in zip(gas, lanes_of)]
        gcs = []
        for last in lasts:
            dh, dm, dl = _split3(jnp.where(eye, jnp.broadcast_to(jnp.exp(last), (2 * c_len, LANES)), 0.0))
            gcs.append(jnp.dot(jnp.concatenate([dh, dm, dl], axis=1), ones_bd3, preferred_element_type=F32))
        for j in tiles:
            q = ci * nlt + j
            ec = jnp.exp(lasts[j] - lgs[j])
            kr_s[q] = krs[j]
            t_s[q] = accs[j][:c_len] + accs[j][c_len:]
            g4_s[q] = jnp.where(incl, gbs[j][c_len:], 0.0)
            gv_s[q] = gvs[j]
            kb_s[q] = jnp.concatenate([kd_ref[rows, lanes_of[j]] * ec, b_ref[rows, lanes_of[j]] * ec], axis=0)
            gc_s[q] = gcs[j]

    for ci in range(n_ch):
        rows = chunk_rows(ci)
        qs = [ci * nlt + j for j in tiles]
        bdss = [s_ref[j] for j in tiles]
        xrs = [_mm3(kr_s[q], bds) for q, bds in zip(qs, bdss)]
        us = [_mm3(t_s[q], two_heads(xr[:c_len] + gv_s[q, :c_len])) for q, xr in zip(qs, xrs)]
        news = [_mm3_tn(kb_s[q], jnp.concatenate([v_ref[rows, sl], -u], axis=0))
                for q, sl, u in zip(qs, lanes_of, us)]
        for j in tiles:
            s_ref[j] = jnp.where(bdmask, news[j], 0.0) + bdss[j] * gc_s[qs[j]]
        for j in tiles:
            y_ref[rows, lanes_of[j]] = (xrs[j][c_len:] + gv_s[qs[j], c_len:]
                                        - _mm3(g4_s[qs[j]], two_heads(us[j])))

    @pl.when(tstep == nt - 1)
    def _():
        for j in range(nlt):
            bds = s_ref[j]
            sf_ref[:, j * LANES:(j + 1) * LANES] = bds[:HEAD_N] + bds[HEAD_N:]


def wkv_scan(r, kk, v, lw, kd, b, s0, row0, n_seq, s_len, d_r):
    tb = min(256, s_len)
    nt = s_len // tb
    blk0 = row0 // tb
    nlt = 4
    wl = nlt * LANES
    nq = (tb // SCAN_C) * nlt

    def tblk(bi, d, c):
        return blk0 + bi * nt + c + d * (nt - 1 - 2 * c)

    shared = pl.BlockSpec((tb, wl), lambda bi, d, hp, c: (tblk(bi, d, c), hp))
    perdir = pl.BlockSpec((None, tb, wl), lambda bi, d, hp, c: (d, tblk(bi, d, c), hp))
    state = pl.BlockSpec((None, None, HEAD_N, wl), lambda bi, d, hp, c: (bi, d, 0, hp))
    kern = functools.partial(_scan_kernel, tb=tb, nt=nt, nlt=nlt)
    return pl.pallas_call(
        kern,
        grid=(n_seq, 2, d_r // wl, nt),
        in_specs=[shared, shared, shared, perdir, perdir, perdir, state],
        out_specs=[pl.BlockSpec((None, tb, wl),
                                lambda bi, d, hp, c: (d, bi * nt + c + d * (nt - 1 - 2 * c), hp)),
                   state],
        out_shape=[jax.ShapeDtypeStruct((2, n_seq * s_len, d_r), F32),
                   jax.ShapeDtypeStruct((n_seq, 2, HEAD_N, d_r), F32)],
        scratch_shapes=[pltpu.VMEM((nlt, 2 * HEAD_N, LANES), F32),
                        pltpu.VMEM((nq, 2 * SCAN_C, LANES), F32),
                        pltpu.VMEM((nq, SCAN_C, LANES), F32),
                        pltpu.VMEM((nq, SCAN_C, LANES), F32),
                        pltpu.VMEM((nq, 2 * SCAN_C, LANES), F32),
                        pltpu.VMEM((nq, 2 * SCAN_C, LANES), F32),
                        pltpu.VMEM((nq, 2 * HEAD_N, LANES), F32)],
        compiler_params=_cparams(("arbitrary", "arbitrary", "arbitrary", "arbitrary")),
        name="wkv_scan",
    )(r, kk, v, lw, kd, b, s0)


def _post_kernel(y_ref, bonus_ref, g_ref, pv_ref, bd_ref, o_ref):
    bd3 = bd_ref[...]
    y = y_ref[0] + y_ref[1]
    mu = _segsum(y, bd3) * (1.0 / HEAD_N)
    yc = y - mu
    var = _segsum(yc * yc, bd3) * (1.0 / HEAD_N)
    yn = yc * lax.rsqrt(var + GN_EPS) * pv_ref[3:4, :] + pv_ref[4:5, :]
    o_ref[...] = (yn + bonus_ref[...]) * g_ref[...]


def rwkv_post(y, bonus, g, pv, bd3, l):
    t, d_r = bonus.shape
    tm = 256
    row = pl.BlockSpec((tm, d_r), lambda i: (i, 0))
    return pl.pallas_call(
        _post_kernel,
        grid=(t // tm,),
        in_specs=[pl.BlockSpec((2, tm, d_r), lambda i: (0, i, 0)), row, row,
                  pl.BlockSpec((None, 8, d_r), lambda i: (l, 0, 0)),
                  pl.BlockSpec((3 * LANES, LANES), lambda i: (0, 0))],
        out_specs=row,
        out_shape=jax.ShapeDtypeStruct((t, d_r), F32),
        compiler_params=_cparams(("arbitrary",)),
        name="rwkv_post",
    )(y, bonus, g, pv, bd3)


def _rope(x, cos, sin):
    lane = lax.broadcasted_iota(jnp.int32, (x.shape[0], LANES), 1)
    first = (lane & 31) < 16
    outs = []
    for j in range(x.shape[1] // LANES):
        xc = x[:, j * LANES:(j + 1) * LANES]
        partner = jnp.where(first, pltpu.roll(xc, LANES - 16, axis=1), pltpu.roll(xc, 16, axis=1))
        outs.append(xc * cos + partner * sin)
    return outs[0] if len(outs) == 1 else jnp.concatenate(outs, axis=1)


def _attn_kernel(lam_ref, q_ref, k_ref, v_ref, *rest, use_ctx, n_heads, out_scale):
    if use_ctx:
        (ck_ref, cv_ref, cosq_ref, sinq_ref, cosk_ref, sink_ref, g_ref,
         o_ref, kb_ref, vb_ref, ckb_ref, cvb_ref) = rest
    else:
        g_ref, o_ref, kb_ref, vb_ref = rest
    qi = pl.program_id(1)

    @pl.when(qi == 0)
    def _():
        k = k_ref[...]
        if use_ctx:
            k = _rope(k, cosk_ref[...], sink_ref[...])
            ckb_ref[...] = ck_ref[...].astype(BF16)
            cvb_ref[...] = cv_ref[...].astype(BF16)
        kb_ref[...] = k.astype(BF16)
        vb_ref[...] = v_ref[...].astype(BF16)

    lam = lam_ref[0]
    q = q_ref[...]
    if use_ctx:
        q = _rope(q, cosq_ref[...], sinq_ref[...])
    tq = q.shape[0]
    lane = lax.broadcasted_iota(jnp.int32, (tq, LANES), 1)
    scale = HEAD_N ** -0.5
    dn = (((1,), (1,)), ((), ()))
    for h in range(n_heads):
        sl = slice(h * LANES, (h + 1) * LANES)
        qh = q[:, sl]
        kh = kb_ref[:, sl]
        probs = []
        for comp in range(2):
            qc = jnp.where((lane < HEAD_N) == (comp == 0), qh, 0.0).astype(BF16)
            s_a = lax.dot_general(qc, kh, dn, preferred_element_type=F32) * scale
            m = jnp.max(s_a, axis=-1, keepdims=True)
            if use_ctx:
                s_b = lax.dot_general(qc, ckb_ref[:, sl], dn, preferred_element_type=F32) * scale
                m = jnp.maximum(m, jnp.max(s_b, axis=-1, keepdims=True))
            e_a = jnp.exp(s_a - m)
            den = jnp.sum(e_a, axis=-1, keepdims=True)
            if use_ctx:
                e_b = jnp.exp(s_b - m)
                den = den + jnp.sum(e_b, axis=-1, keepdims=True)
                probs.append((e_a / den, e_b / den))
            else:
                probs.append((e_a / den,))
        pd = [p1 - lam * p2 for p1, p2 in zip(*probs)]
        o = jnp.dot(pd[0].astype(BF16), vb_ref[:, sl], preferred_element_type=F32)
        if use_ctx:
            o = o + jnp.dot(pd[1].astype(BF16), cvb_ref[:, sl], preferred_element_type=F32)
        o_ref[:, sl] = _rms(o, g_ref[...]) * out_scale


def diff_attn(z, lam, subln3, l, lam_init, row0, n_seq, s_len, d_diff, col0, ctx=None):
    tq = 128
    nq = s_len // tq
    n_heads = d_diff // LANES
    qb0 = row0 // tq
    sb0 = row0 // s_len
    use_ctx = ctx is not None
    in_specs = [pl.BlockSpec(memory_space=pltpu.SMEM),
                pl.BlockSpec((tq, d_diff), lambda bi, qi: (qb0 + bi * nq + qi, col0)),
                pl.BlockSpec((s_len, d_diff), lambda bi, qi: (sb0 + bi, col0 + 1)),
                pl.BlockSpec((s_len, d_diff), lambda bi, qi: (sb0 + bi, col0 + 2))]
    args = [lam, z, z, z]
    scratch = [pltpu.VMEM((s_len, d_diff), BF16), pltpu.VMEM((s_len, d_diff), BF16)]
    if use_ctx:
        cache_k, cache_v, cos, sin = ctx
        p_len = cache_k.shape[2]
        cspec = pl.BlockSpec((None, None, p_len, d_diff), lambda bi, qi: (bi, l, 0, 0))
        in_specs += [cspec, cspec,
                     pl.BlockSpec((tq, LANES), lambda bi, qi: (qi, 0)),
                     pl.BlockSpec((tq, LANES), lambda bi, qi: (qi, 0)),
                     pl.BlockSpec((s_len, LANES), lambda bi, qi: (0, 0)),
                     pl.BlockSpec((s_len, LANES), lambda bi, qi: (0, 0))]
        args += [cache_k, cache_v, cos, sin, cos, sin]
        scratch += [pltpu.VMEM((p_len, d_diff), BF16), pltpu.VMEM((p_len, d_diff), BF16)]
    in_specs.append(pl.BlockSpec((None, 1, LANES), lambda bi, qi: (l, 0, 0)))
    args.append(subln3)
    kern = functools.partial(_attn_kernel, use_ctx=use_ctx, n_heads=n_heads,
                             out_scale=1.0 - lam_init)
    return pl.pallas_call(
        kern,
        grid=(n_seq, nq),
        in_specs=in_specs,
        out_specs=pl.BlockSpec((tq, d_diff), lambda bi, qi: (bi * nq + qi, 0)),
        out_shape=jax.ShapeDtypeStruct((n_seq * s_len, d_diff), F32),
        scratch_shapes=scratch,
        compiler_params=_cparams(("arbitrary", "arbitrary")),
        name="diff_attn_ctx" if use_ctx else "diff_attn",
    )(*args)


def _chunk_kernel(u_ref, g_ref, vn_ref, ws_ref, bs_ref, o_ref):
    u = jax.nn.gelu(u_ref[...], approximate=True)
    vv = _rms(jax.nn.gelu(g_ref[...], approximate=True), vn_ref[...])
    for gi in range(ws_ref.shape[0]):
        sl = slice(gi * LANES, (gi + 1) * LANES)
        s = jnp.dot(ws_ref[gi].astype(BF16), vv[:, sl].astype(BF16), preferred_element_type=F32)
        o_ref[:, sl] = u[:, sl] * (s + bs_ref[:, sl])


def chunk_mix(z, vn3, chunk_ws, bs_exp, l, d_c, col_u):
    t = z.shape[0]
    n_g = chunk_ws.shape[1]
    return pl.pallas_call(
        _chunk_kernel,
        grid=(t // CHUNK,),
        in_specs=[pl.BlockSpec((CHUNK, d_c), lambda i: (i, col_u)),
                  pl.BlockSpec((CHUNK, d_c), lambda i: (i, col_u + 1)),
                  pl.BlockSpec((None, 1, d_c), lambda i: (l, 0, 0)),
                  pl.BlockSpec((None, n_g, CHUNK, CHUNK), lambda i: (l, 0, 0, 0)),
                  pl.BlockSpec((None, CHUNK, d_c), lambda i: (l, 0, 0))],
        out_specs=pl.BlockSpec((CHUNK, d_c), lambda i: (i, 0)),
        out_shape=jax.ShapeDtypeStruct((t, d_c), F32),
        compiler_params=_cparams(("arbitrary",)),
        name="chunk_mix",
    )(z, z, vn3, chunk_ws, bs_exp)


def _outproj_kernel(x_ref, or_ref, od_ref, oc_ref, wr_ref, wd_ref, wc_ref, g1n_ref, g2n_ref,
                    gate_ref, sh_ref, sc_ref, rw_ref, rb_ref, x1_ref, h2_ref, lg_ref):
    mix = jnp.dot(or_ref[...].astype(BF16), wr_ref[...], preferred_element_type=F32)
    mix = mix + jnp.dot(od_ref[...].astype(BF16), wd_ref[...], preferred_element_type=F32)
    mix = mix + jnp.dot(oc_ref[...].astype(BF16), wc_ref[...], preferred_element_type=F32)
    x1 = x_ref[...] + gate_ref[...] * _rms(mix, g1n_ref[...])
    x1_ref[...] = x1
    h2 = _rms(x1, g2n_ref[...]) * (1.0 + sc_ref[...]) + sh_ref[...]
    h2_ref[...] = h2
    hh, hm, hl = _split3(h2)
    wh, wm, wl = _split3(rw_ref[...])
    acc = jnp.dot(hh, wh, preferred_element_type=F32)
    for a, b in ((hh, wm), (hm, wh), (hm, wm), (hh, wl), (hl, wh)):
        acc = acc + jnp.dot(a, b, preferred_element_type=F32)
    lg_ref[...] = acc + rb_ref[...]


def out_proj(x, o_r, o_d, o_c, w_out_b, norm_g3, mods4, rw_p, rb_p, l, t_ctx, s_dec):
    t, d = x.shape
    d_r, d_d, d_c = o_r.shape[1], o_d.shape[1], o_c.shape[1]
    tm = math.gcd(256, t_ctx, s_dec)
    mrow = lambda i: _mod_row(i, tm, t_ctx, s_dec)
    nrm = lambda k: pl.BlockSpec((None, 1, d), lambda i: (4 * l + k, 0, 0))
    mod = lambda k: pl.BlockSpec((None, None, 1, d), lambda i: (mrow(i), k, 0, 0))
    row = lambda w: pl.BlockSpec((tm, w), lambda i: (i, 0))
    nrb = d_r // d_d
    return pl.pallas_call(
        _outproj_kernel,
        grid=(t // tm,),
        in_specs=[row(d), row(d_r), row(d_d), row(d_c),
                  pl.BlockSpec((None, d_r, d), lambda i: (l, 0, 0)),
                  pl.BlockSpec((None, d_d, d), lambda i: (l, nrb, 0)),
                  pl.BlockSpec((None, d_c, d), lambda i: (l, nrb + 1, 0)),
                  nrm(1), nrm(2), mod(2), mod(3), mod(4),
                  pl.BlockSpec((None, d, LANES), lambda i: (l, 0, 0)),
                  pl.BlockSpec((None, 1, LANES), lambda i: (l, 0, 0))],
        out_specs=[row(d), row(d), row(LANES)],
        out_shape=[jax.ShapeDtypeStruct((t, d), F32), jax.ShapeDtypeStruct((t, d), F32),
                   jax.ShapeDtypeStruct((t, LANES), F32)],
        compiler_params=_cparams(("arbitrary",)),
        name="out_proj",
    )(x, o_r, o_d, o_c, w_out_b, w_out_b, w_out_b, norm_g3, norm_g3, mods4, mods4, mods4, rw_p, rb_p)


def _gather_kernel(idx_ref, src_ref, o_ref, sem, *, bm):
    def row_copy(r, src_row):
        return pltpu.make_async_copy(src_ref.at[pl.ds(src_row, 1), :], o_ref.at[pl.ds(r, 1), :], sem)

    def issue(r, carry):
        row_copy(r, idx_ref[0, r]).start()
        return carry

    lax.fori_loop(0, bm, issue, 0)
    pltpu.make_async_copy(o_ref, o_ref, sem).wait()


def gather_rows(src, idx, bm=128):
    n, d = src.shape
    m = idx.shape[0]
    return pl.pallas_call(
        functools.partial(_gather_kernel, bm=bm),
        grid=(m // bm,),
        in_specs=[pl.BlockSpec((None, 1, bm), lambda i: (i, 0, 0), memory_space=pltpu.SMEM),
                  pl.BlockSpec(memory_space=pl.ANY)],
        out_specs=pl.BlockSpec((bm, d), lambda i: (i, 0)),
        out_shape=jax.ShapeDtypeStruct((m, d), src.dtype),
        scratch_shapes=[pltpu.SemaphoreType.DMA],
        compiler_params=_cparams(("arbitrary",)),
        name="gather_rows",
    )(idx.reshape(m // bm, 1, bm), src)


def _expert_kernel(be_ref, bv_ref, x_ref, wg_ref, wu_ref, bg_ref, bu_ref, wd_ref, bdn_ref,
                   o_ref, xb_ref, *, nf):
    i = pl.program_id(0)
    f = pl.program_id(1)

    @pl.when(f == 0)
    def _():
        xb_ref[...] = x_ref[...].astype(BF16)
        o_ref[...] = jnp.zeros_like(o_ref)

    @pl.when(bv_ref[i] > 0)
    def _():
        xb = xb_ref[...]
        gt = jnp.dot(xb, wg_ref[...], preferred_element_type=F32) + bg_ref[...]
        up = jnp.dot(xb, wu_ref[...], preferred_element_type=F32) + bu_ref[...]
        gt = jnp.minimum(gt, SWIGLU_LIMIT)
        up = jnp.clip(up, -SWIGLU_LIMIT, SWIGLU_LIMIT)
        act = (up + 1.0) * gt * jax.nn.sigmoid(SWIGLU_ALPHA * gt)
        o_ref[...] += jnp.dot(act.astype(BF16), wd_ref[...], preferred_element_type=F32)

    @pl.when(f == nf - 1)
    def _():
        o_ref[...] = o_ref[...] + bdn_ref[...]


def expert_ffn(xs, block_e, block_valid, w_gu_b, b_gu3, w_dn_b, b_dn3, l, n_e, bm):
    n_slots, d = xs.shape
    ff = w_dn_b.shape[1]
    ft = 512
    nf = ff // ft
    nb = n_slots // bm
    e0 = l * n_e

    def fsel(i, f, bv):
        return jnp.where(bv[i] > 0, f, nf - 1)

    return pl.pallas_call(
        functools.partial(_expert_kernel, nf=nf),
        grid_spec=pltpu.PrefetchScalarGridSpec(
            num_scalar_prefetch=2,
            grid=(nb, nf),
            in_specs=[pl.BlockSpec((bm, d), lambda i, f, be, bv: (i, 0)),
                      pl.BlockSpec((None, d, ft), lambda i, f, be, bv: (e0 + be[i], 0, fsel(i, f, bv))),
                      pl.BlockSpec((None, d, ft), lambda i, f, be, bv: (e0 + be[i], 0, nf + fsel(i, f, bv))),
                      pl.BlockSpec((None, 1, ft), lambda i, f, be, bv: (e0 + be[i], 0, fsel(i, f, bv))),
                      pl.BlockSpec((None, 1, ft), lambda i, f, be, bv: (e0 + be[i], 0, nf + fsel(i, f, bv))),
                      pl.BlockSpec((None, ft, d), lambda i, f, be, bv: (e0 + be[i], fsel(i, f, bv), 0)),
                      pl.BlockSpec((None, 1, d), lambda i, f, be, bv: (e0 + be[i], 0, 0))],
            out_specs=pl.BlockSpec((bm, d), lambda i, f, be, bv: (i, 0)),
            scratch_shapes=[pltpu.VMEM((bm, d), BF16)]),
        out_shape=jax.ShapeDtypeStruct((n_slots, d), F32),
        compiler_params=_cparams(("arbitrary", "arbitrary")),
        name="expert_ffn",
    )(block_e, block_valid, xs, w_gu_b, w_gu_b, b_gu3, b_gu3, w_dn_b, b_dn3)


def _combine_kernel(x_ref, y_ref, rg_ref, gn_ref, gate_ref, o_ref, *, d):
    f = y_ref[:, 0:d] * rg_ref[:, 0:1]
    for k in range(1, TOP_K):
        f = f + y_ref[:, k * d:(k + 1) * d] * rg_ref[:, k:k + 1]
    o_ref[...] = x_ref[...] + gate_ref[...] * _rms(f, gn_ref[...])


def moe_combine(x1, yk, gates, norm_g3, mods4, l, t_ctx, s_dec):
    t, d = x1.shape
    tm = 128
    mrow = lambda i: _mod_row(i, tm, t_ctx, s_dec)
    return pl.pallas_call(
        functools.partial(_combine_kernel, d=d),
        grid=(t // tm,),
        in_specs=[pl.BlockSpec((tm, d), lambda i: (i, 0)),
                  pl.BlockSpec((tm, TOP_K * d), lambda i: (i, 0)),
                  pl.BlockSpec((tm, TOP_K), lambda i: (i, 0)),
                  pl.BlockSpec((None, 1, d), lambda i: (4 * l + 3, 0, 0)),
                  pl.BlockSpec((None, None, 1, d), lambda i: (mrow(i), 5, 0, 0))],
        out_specs=pl.BlockSpec((tm, d), lambda i: (i, 0)),
        out_shape=jax.ShapeDtypeStruct((t, d), F32),
        compiler_params=_cparams(("arbitrary",)),
        name="moe_combine",
    )(x1, yk, gates, norm_g3, mods4)


def _route(logits, n_e, bm):
    t = logits.shape[0]
    top_v, top_i = lax.top_k(logits, TOP_K)
    gates = jax.nn.softmax(top_v, axis=-1)
    tk = t * TOP_K
    flat_e = top_i.reshape(-1).astype(jnp.int32)
    eids = jnp.arange(n_e, dtype=jnp.int32)
    iota = jnp.arange(tk, dtype=jnp.int32)
    sorted_e, order = lax.sort_key_val(flat_e, iota)
    counts = jnp.sum((flat_e[:, None] == eids[None, :]).astype(jnp.int32), axis=0)
    padded = ((counts + bm - 1) // bm) * bm
    pad_end = jnp.cumsum(padded)
    pad_start = pad_end - padded
    start = jnp.cumsum(counts) - counts
    shift = pad_start - start
    dest = iota + jnp.sum(jnp.where(sorted_e[:, None] == eids[None, :], shift[None, :], 0), axis=1)
    _, pos = lax.sort_key_val(order, dest)
    nb = (tk + n_e * (bm - 1) + bm - 1) // bm
    starts = jnp.arange(nb, dtype=jnp.int32) * bm
    block_e = jnp.minimum(jnp.searchsorted(pad_end, starts, side='right'), n_e - 1).astype(jnp.int32)
    block_valid = (starts < pad_end[-1]).astype(jnp.int32)
    src = starts[:, None] + jnp.arange(bm, dtype=jnp.int32)[None, :] - shift[block_e][:, None]
    live = jnp.logical_and(src < (start + counts)[block_e][:, None], block_valid[:, None] > 0)
    slot_tok = jnp.where(live, order[jnp.clip(src, 0, tk - 1)] // TOP_K, 0).reshape(-1)
    last_e = jnp.max(jnp.where(counts > 0, eids, 0))
    block_e = jnp.where(block_valid > 0, block_e, last_e)
    return slot_tok, gates, block_e, block_valid, pos


def _rope_tables(s_len):
    rows = s_len // GRID_W
    row = jnp.repeat(jnp.arange(rows), GRID_W)
    col = jnp.tile(jnp.arange(GRID_W), rows)
    half = HEAD_N // 2
    inv_freq = 1.0 / (ROPE_THETA ** (jnp.arange(0, half, 2, dtype=F32) / half))
    ang = jnp.stack([row, col], axis=-1).astype(F32)[..., None] * inv_freq
    cos, sin = jnp.cos(ang), jnp.sin(ang)
    cos64 = jnp.concatenate([cos[:, 0], cos[:, 0], cos[:, 1], cos[:, 1]], axis=-1)
    sin64 = jnp.concatenate([-sin[:, 0], sin[:, 0], -sin[:, 1], sin[:, 1]], axis=-1)
    return jnp.tile(cos64, (1, 2)), jnp.tile(sin64, (1, 2))


def kernel(x_prompt, x_sample, c, cache_k, cache_v, state_wkv, c_ctx, ada_w, ada_b, norm_g, w_in, shift_mu, decay_w0, decay_w2, iclr_a0, iclr_a2, gate_g2, k_k, k_a, r_k, ln_x_w, ln_x_b, diff_lam, diff_subln, chunk_vnorm, chunk_ws, chunk_bs, w_out, router_w, router_b, w_gu, b_gu, w_dn, b_dn):
    b_ctx, s_ctx, d = x_prompt.shape
    b_dec, s_dec, _ = x_sample.shape
    depth = ada_w.shape[0]
    t_ctx, t_dec = b_ctx * s_ctx, b_dec * s_dec
    d_r = k_k.shape[1]
    n_h = d_r // HEAD_N
    d_rin = shift_mu.shape[2]
    d_diff = diff_subln.shape[1] * (cache_k.shape[3])
    n_g = chunk_ws.shape[1]
    d_c = n_g * chunk_ws.shape[2]
    n_e = router_w.shape[2]
    d_lw, d_la, d_lg = decay_w2.shape[2], iclr_a2.shape[2], gate_g2.shape[1]
    p_len = cache_k.shape[2]
    assert d_lw == HEAD_N and d_la == HEAD_N and d_lg <= 2 * LANES and d_diff == d_c
    wz = 3 * d_r + 4 * LANES
    col_q = wz // d_diff
    assert wz % d_diff == 0 and d_rin <= wz

    padc = wz - d_rin
    w_in_p = jnp.concatenate([w_in[:, :, :d_rin], jnp.zeros((depth, d, padc), F32), w_in[:, :, d_rin:]],
                             axis=2).astype(BF16)
    mu_p = jnp.pad(shift_mu, ((0, 0), (0, 6), (0, padc)))
    pv = jnp.stack([k_k, k_a, r_k.reshape(depth, d_r), ln_x_w, ln_x_b], axis=1)
    pv = jnp.pad(pv, ((0, 0), (0, 3), (0, 0)))
    p2 = jnp.stack([decay_w0.reshape(depth, 2 * d_r), iclr_a0.reshape(depth, 2 * d_r)], axis=1)
    p2 = jnp.pad(p2, ((0, 0), (0, 6), (0, 0)))
    zl = jnp.zeros((depth, HEAD_N, d_r), F32)
    w2bd = jnp.concatenate([jnp.concatenate([decay_w2[:, 0], zl], axis=2),
                            jnp.concatenate([zl, decay_w2[:, 1]], axis=2)], axis=1).astype(BF16)
    a2bd = jnp.concatenate([jnp.concatenate([iclr_a2[:, 0], zl], axis=2),
                            jnp.concatenate([zl, iclr_a2[:, 1]], axis=2)], axis=1).astype(BF16)
    g2p = jnp.pad(gate_g2, ((0, 0), (0, 2 * LANES - d_lg), (0, 0))).astype(BF16)
    lane = jnp.arange(LANES)
    bd = (lane[:, None] // HEAD_N == lane[None, :] // HEAD_N).astype(BF16)
    bd3 = jnp.concatenate([bd, bd, bd], axis=0)
    norm_g3 = norm_g.reshape(depth * 4, 1, d)
    ada_b3 = ada_b.reshape(depth, 1, 6 * d)
    subln3 = diff_subln.reshape(depth, 1, LANES)
    vn3 = chunk_vnorm.reshape(depth, 1, d_c)
    bs_exp = jnp.repeat(jnp.swapaxes(chunk_bs, 1, 2), d_c // n_g, axis=2)
    w_out_b = w_out.astype(BF16)
    rw_p = jnp.pad(router_w, ((0, 0), (0, 0), (0, LANES - n_e)))
    rb_p = jnp.pad(router_b, ((0, 0), (0, LANES - n_e))).reshape(depth, 1, LANES)
    w_gu_b = w_gu.astype(BF16).reshape(depth * n_e, d, -1)
    w_dn_b = w_dn.astype(BF16).reshape(depth * n_e, -1, d)
    b_gu3 = b_gu.reshape(depth * n_e, 1, -1)
    b_dn3 = b_dn.reshape(depth * n_e, 1, d)
    cos_t, sin_t = _rope_tables(s_dec)
    ck4 = cache_k.reshape(b_dec, depth, p_len, d_diff)
    cv4 = cache_v.reshape(b_dec, depth, p_len, d_diff)
    s0_dec = jnp.transpose(state_wkv, (0, 1, 2, 5, 3, 4)).reshape(b_dec, depth, 2, HEAD_N, d_r)
    s0_ctx = jnp.zeros((b_ctx, 2, HEAD_N, d_r), F32)

    cond = jnp.concatenate([c_ctx[None, :], c], axis=0)
    m_rows = -(-cond.shape[0] // 8) * 8
    cond = jnp.pad(cond, ((0, m_rows - cond.shape[0]), (0, 0)))

    x = jnp.concatenate([x_prompt.reshape(t_ctx, d), x_sample.reshape(t_dec, d)], axis=0)
    bm = 512
    ks, vs, ss = [], [], []
    for l in range(depth):
        mods4 = ada_mods(cond, ada_w, ada_b3, l).reshape(m_rows, 6, 1, d)
        z = in_proj(x, mods4, norm_g3, w_in_p, l, t_ctx, s_dec)
        ks.append(z[:t_ctx, (col_q + 1) * d_diff:(col_q + 2) * d_diff])
        vs.append(z[:t_ctx, (col_q + 2) * d_diff:(col_q + 3) * d_diff])

        r, v, kk, w, kd, b, bonus, g = rwkv_prep(z, mu_p, pv, p2, w2bd, a2bd, g2p, bd3, l,
                                                 t_ctx, s_ctx, s_dec, d_r)
        y_c, sf_c = wkv_scan(r, kk, v, w, kd, b, s0_ctx, 0, b_ctx, s_ctx, d_r)
        y_d, _ = wkv_scan(r, kk, v, w, kd, b, s0_dec[:, l], t_ctx, b_dec, s_dec, d_r)
        ss.append(sf_c)
        o_r = rwkv_post(jnp.concatenate([y_c, y_d], axis=1), bonus, g, pv, bd3, l)

        lam_init = 0.8 - 0.6 * math.exp(-0.3 * l)
        lq1, lk1, lq2, lk2 = diff_lam[l].astype(F32)
        lam = (jnp.exp(jnp.sum(lq1 * lk1)) - jnp.exp(jnp.sum(lq2 * lk2)) + lam_init).reshape(1)
        od_c = diff_attn(z, lam, subln3, l, lam_init, 0, b_ctx, s_ctx, d_diff, col_q)
        od_d = diff_attn(z, lam, subln3, l, lam_init, t_ctx, b_dec, s_dec, d_diff, col_q,
                         ctx=(ck4, cv4, cos_t, sin_t))
        o_d = jnp.concatenate([od_c, od_d], axis=0)
        o_c = chunk_mix(z, vn3, chunk_ws, bs_exp, l, d_c, col_q + 3)

        x1, h2, logits = out_proj(x, o_r, o_d, o_c, w_out_b, norm_g3, mods4, rw_p, rb_p, l, t_ctx, s_dec)
        slot_tok, gates, block_e, block_valid, pos = _route(logits[:, :n_e], n_e, bm)
        xs = gather_rows(h2, slot_tok)
        ys = expert_ffn(xs, block_e, block_valid, w_gu_b, b_gu3, w_dn_b, b_dn3, l, n_e, bm)
        yk = gather_rows(ys, pos).reshape(t_ctx + t_dec, TOP_K * d)
        x = moe_combine(x1, yk, gates, norm_g3, mods4, l, t_ctx, s_dec)

    n_hd = cache_k.shape[3]
    new_k = jnp.stack(ks, axis=1).reshape(b_ctx, s_ctx, depth, n_hd, 2, HEAD_N)
    new_k = jnp.transpose(new_k, (0, 2, 1, 3, 4, 5))
    new_v = jnp.stack(vs, axis=1).reshape(b_ctx, s_ctx, depth, n_hd, 2 * HEAD_N)
    new_v = jnp.transpose(new_v, (0, 2, 1, 3, 4))
    st = jnp.stack(ss, axis=1).reshape(b_ctx, depth, 2, HEAD_N, n_h, HEAD_N)
    new_s = jnp.transpose(st, (0, 1, 2, 4, 5, 3))
    y_prompt = x[:t_ctx].reshape(b_ctx, s_ctx, d)
    y_sample = x[t_ctx:].reshape(b_dec, s_dec, d)
    return (y_prompt, y_sample, new_k, new_v, new_s)
```

```python
import functools
import math

import jax
import jax.numpy as jnp
from jax import lax
from jax.experimental import pallas as pl
from jax.experimental.pallas import tpu as pltpu

F32 = jnp.float32
BF16 = jnp.bfloat16

LANES = 128
HEAD_N = 64
NORM_EPS = 1e-6
GN_EPS = 64e-5
ROPE_THETA = 10000.0
GRID_W = 64
TOP_K = 4
SWIGLU_LIMIT = 7.0
SWIGLU_ALPHA = 1.702
CHUNK = 128
VMEM_LIMIT = 56 * 1024 * 1024


def _cparams(sem):
    return pltpu.CompilerParams(dimension_semantics=sem, vmem_limit_bytes=VMEM_LIMIT)


def _split3(x):
    hi = x.astype(BF16)
    r1 = x - hi.astype(F32)
    mid = r1.astype(BF16)
    lo = (r1 - mid.astype(F32)).astype(BF16)
    return hi, mid, lo


def _segsum(x, bd3):
    outs = []
    for j in range(x.shape[1] // LANES):
        hi, mid, lo = _split3(x[:, j * LANES:(j + 1) * LANES])
        lhs = jnp.concatenate([hi, mid, lo], axis=1)
        outs.append(jnp.dot(lhs, bd3, preferred_element_type=F32))
    return outs[0] if len(outs) == 1 else jnp.concatenate(outs, axis=1)


def _rms(x, g):
    return x * lax.rsqrt(jnp.mean(x * x, axis=-1, keepdims=True) + NORM_EPS) * g


def _ada_kernel(c_ref, w_ref, b_ref, o_ref):
    c = c_ref[...]
    s = c * jax.nn.sigmoid(c)
    o_ref[...] = jnp.dot(s.astype(BF16), w_ref[...].astype(BF16),
                         preferred_element_type=F32) + b_ref[...]


def ada_mods(cond, ada_w, ada_b3, l):
    m, d = cond.shape
    n = ada_w.shape[2]
    tn = 1024
    return pl.pallas_call(
        _ada_kernel,
        grid=(n // tn,),
        in_specs=[pl.BlockSpec((m, d), lambda j: (0, 0)),
                  pl.BlockSpec((None, d, tn), lambda j: (l, 0, j)),
                  pl.BlockSpec((None, 1, tn), lambda j: (l, 0, j))],
        out_specs=pl.BlockSpec((m, tn), lambda j: (0, j)),
        out_shape=jax.ShapeDtypeStruct((m, n), F32),
        compiler_params=_cparams(("arbitrary",)),
        name="ada_mods",
    )(cond, ada_w, ada_b3)


def _inproj_kernel(x_ref, g_ref, sh_ref, sc_ref, w_ref, o_ref, h_ref):
    @pl.when(pl.program_id(1) == 0)
    def _():
        h = _rms(x_ref[...], g_ref[...]) * (1.0 + sc_ref[...]) + sh_ref[...]
        h_ref[...] = h.astype(BF16)

    o_ref[...] = jnp.dot(h_ref[...], w_ref[...], preferred_element_type=F32)


def _mod_row(i, tm, t_ctx, s_dec):
    r0 = i * tm
    return jnp.where(r0 < t_ctx, 0, 1 + (r0 - t_ctx) // s_dec)


def in_proj(x, mods4, norm_g3, w_in_p, l, t_ctx, s_dec):
    t, d = x.shape
    n = w_in_p.shape[2]
    tm, tn = math.gcd(512, t_ctx, s_dec), 512
    mrow = lambda i: _mod_row(i, tm, t_ctx, s_dec)
    return pl.pallas_call(
        _inproj_kernel,
        grid=(t // tm, n // tn),
        in_specs=[pl.BlockSpec((tm, d), lambda i, j: (i, 0)),
                  pl.BlockSpec((None, 1, d), lambda i, j: (4 * l, 0, 0)),
                  pl.BlockSpec((None, None, 1, d), lambda i, j: (mrow(i), 0, 0, 0)),
                  pl.BlockSpec((None, None, 1, d), lambda i, j: (mrow(i), 1, 0, 0)),
                  pl.BlockSpec((None, d, tn), lambda i, j: (l, 0, j))],
        out_specs=pl.BlockSpec((tm, tn), lambda i, j: (i, j)),
        out_shape=jax.ShapeDtypeStruct((t, n), F32),
        scratch_shapes=[pltpu.VMEM((tm, d), BF16)],
        compiler_params=_cparams(("arbitrary", "arbitrary")),
        name="in_proj",
    )(x, norm_g3, mods4, mods4, w_in_p)


def _prep_kernel(z_ref, zp_ref, zn_ref, mu_ref, pv_ref, p2_ref, w2_ref, a2_ref, g2_ref, bd_ref,
                 r_ref, v_ref, kk_ref, w_ref, kd_ref, b_ref, bonus_ref, g_ref, zs_ref,
                 *, tm, t_ctx, s_ctx, s_dec, d_r):
    i = pl.program_id(0)
    r0 = i * tm
    in_ctx = r0 < t_ctx
    pos0 = jnp.where(in_ctx, r0 % s_ctx, (r0 - t_ctx) % s_dec)
    slen = jnp.where(in_ctx, s_ctx, s_dec)
    is_start = pos0 == 0
    is_end = pos0 + tm == slen
    width = z_ref.shape[1]
    rows = lax.broadcasted_iota(jnp.int32, (tm, LANES), 0)
    for j in range(width // LANES):
        sl = slice(j * LANES, (j + 1) * LANES)
        zc = z_ref[:, sl]
        prev_row = jnp.where(is_start, 0.0, zp_ref[7:8, sl])
        next_row = jnp.where(is_end, 0.0, zn_ref[0:1, sl])
        zprev = jnp.where(rows == 0, prev_row, pltpu.roll(zc, 1, axis=0))
        znext = jnp.where(rows == tm - 1, next_row, pltpu.roll(zc, tm - 1, axis=0))
        zs_ref[:, sl] = zc + mu_ref[0:1, sl] * (zprev - zc) + mu_ref[1:2, sl] * (znext - zc)

    bd3 = bd_ref[...]
    r = zs_ref[:, 0:d_r]
    k = zs_ref[:, d_r:2 * d_r]
    v = zs_ref[:, 2 * d_r:3 * d_r]
    o = 3 * d_r
    cw = zs_ref[:, o:o + LANES]
    ca = zs_ref[:, o + LANES:o + 2 * LANES]
    cg = zs_ref[:, o + 2 * LANES:o + 4 * LANES]
    k_k = pv_ref[0:1, :]
    k_a = pv_ref[1:2, :]
    r_k = pv_ref[2:3, :]

    r_ref[...] = r
    v_ref[...] = v
    g_ref[...] = jnp.dot(jax.nn.sigmoid(cg).astype(BF16), g2_ref[...], preferred_element_type=F32)

    kkr = k * k_k
    nrm = jnp.sqrt(_segsum(kkr * kkr, bd3))
    kk = kkr / jnp.maximum(nrm, 1e-12)
    kk_ref[...] = kk

    lw = p2_ref[0:1, :] + jnp.dot(jnp.tanh(cw).astype(BF16), w2_ref[...], preferred_element_type=F32)
    decay = -math.exp(-0.5) * jax.nn.sigmoid(lw)
    a = jax.nn.sigmoid(p2_ref[1:2, :] + jnp.dot(ca.astype(BF16), a2_ref[...], preferred_element_type=F32))
    kd_sum = None
    for d in range(2):
        a_d = a[:, d * d_r:(d + 1) * d_r]
        kd = k * (1.0 + (a_d - 1.0) * k_a)
        w_ref[d] = decay[:, d * d_r:(d + 1) * d_r]
        kd_ref[d] = kd
        b_ref[d] = kk * a_d
        kd_sum = kd if kd_sum is None else kd_sum + kd
    bonus_ref[...] = _segsum(r * kd_sum * r_k, bd3) * v


def rwkv_prep(z, mu_p, pv, p2, w2bd, a2bd, g2p, bd3, l, t_ctx, s_ctx, s_dec, d_r):
    t = z.shape[0]
    tm = 128
    wz = mu_p.shape[2]
    nb8 = t // 8
    kern = functools.partial(_prep_kernel, tm=tm, t_ctx=t_ctx, s_ctx=s_ctx, s_dec=s_dec, d_r=d_r)
    row = pl.BlockSpec((tm, d_r), lambda i: (i, 0))
    row2 = pl.BlockSpec((2, tm, d_r), lambda i: (0, i, 0))
    o1 = jax.ShapeDtypeStruct((t, d_r), F32)
    o2 = jax.ShapeDtypeStruct((2, t, d_r), F32)
    return pl.pallas_call(
        kern,
        grid=(t // tm,),
        in_specs=[pl.BlockSpec((tm, wz), lambda i: (i, 0)),
                  pl.BlockSpec((8, wz), lambda i: (jnp.maximum(i * (tm // 8) - 1, 0), 0)),
                  pl.BlockSpec((8, wz), lambda i: (jnp.minimum((i + 1) * (tm // 8), nb8 - 1), 0)),
                  pl.BlockSpec((None, 8, wz), lambda i: (l, 0, 0)),
                  pl.BlockSpec((None, 8, d_r), lambda i: (l, 0, 0)),
                  pl.BlockSpec((None, 8, 2 * d_r), lambda i: (l, 0, 0)),
                  pl.BlockSpec((None, LANES, 2 * d_r), lambda i: (l, 0, 0)),
                  pl.BlockSpec((None, LANES, 2 * d_r), lambda i: (l, 0, 0)),
                  pl.BlockSpec((None, 2 * LANES, d_r), lambda i: (l, 0, 0)),
                  pl.BlockSpec((3 * LANES, LANES), lambda i: (0, 0))],
        out_specs=[row, row, row, row2, row2, row2, row, row],
        out_shape=[o1, o1, o1, o2, o2, o2, o1, o1],
        scratch_shapes=[pltpu.VMEM((tm, wz), F32)],
        compiler_params=_cparams(("arbitrary",)),
        name="rwkv_prep",
    )(z, z, z, mu_p, pv, p2, w2bd, a2bd, g2p, bd3)


def _split2(x):
    hi = x.astype(BF16)
    lo = (x - hi.astype(F32)).astype(BF16)
    return hi, lo


def _lhs3(a, axis):
    ah, al = _split2(a)
    return jnp.concatenate([ah, ah, al], axis=axis)


def _rhs3(b, axis):
    bh, bl = _split2(b)
    return jnp.concatenate([bh, bl, bh], axis=axis)


def _dot3(l3, r3):
    return jnp.dot(l3, r3, preferred_element_type=F32)


def _dot3_nt(l3, r3):
    return lax.dot_general(l3, r3, (((1,), (1,)), ((), ())), preferred_element_type=F32)


def _dot3_tn(l3, r3):
    return lax.dot_general(l3, r3, (((0,), (0,)), ((), ())), preferred_element_type=F32)


def _mm1(a, b):
    return jnp.dot(a.astype(BF16), b.astype(BF16), preferred_element_type=F32)


SCAN_C = 64


def _scan_kernel(r_ref, kk_ref, v_ref, lw_ref, kd_ref, b_ref, s0_ref,
                 y_ref, sf_ref, s_ref, kr_s, t_s, g4_s, gv_s, kb_s, gc_s, *, tb, nt, nlt):
    c_len = SCAN_C
    fwd = pl.program_id(1) == 0
    tstep = pl.program_id(3)
    lane = lax.broadcasted_iota(jnp.int32, (c_len, LANES), 1)
    row = lax.broadcasted_iota(jnp.int32, (c_len, LANES), 0)
    colm = lane & (HEAD_N - 1)
    ahead = (row - colm) * jnp.where(fwd, 1, -1)
    strict = ahead > 0
    incl = ahead >= 0
    h0 = lane < HEAD_N
    lane2 = lax.broadcasted_iota(jnp.int32, (2 * c_len, LANES), 1)
    row2 = lax.broadcasted_iota(jnp.int32, (2 * c_len, LANES), 0)
    bdmask = (lane2 < HEAD_N) == (row2 < HEAD_N)
    eye = lane2 == row2
    m_incl = jnp.where(jnp.logical_and(incl, h0), 1.0, 0.0).astype(BF16)
    m_incl3 = jnp.concatenate([m_incl, m_incl, m_incl], axis=1)
    zpad = jnp.zeros((c_len, LANES), BF16)

    def two_heads(x):
        return jnp.concatenate([jnp.where(h0, x, 0.0), jnp.where(h0, 0.0, x)], axis=0)

    @pl.when(tstep == 0)
    def _():
        for j in range(nlt):
            s_ref[j] = two_heads(s0_ref[:, j * LANES:(j + 1) * LANES])

    n_ch = tb // c_len

    def chunk_rows(ci):
        cc = jnp.where(fwd, ci, n_ch - 1 - ci)
        return pl.ds(pl.multiple_of(cc * c_len, c_len), c_len)

    tiles = range(nlt)
    lanes_of = [slice(j * LANES, (j + 1) * LANES) for j in tiles]
    eye_f = jnp.where(eye, 1.0, 0.0)
    for ci in range(n_ch):
        rows = chunk_rows(ci)
        lws = [lw_ref[rows, sl] for sl in lanes_of]
        lgs = []
        for lw in lws:
            hi, mid, lo = _split3(lw)
            lgs.append(jnp.dot(m_incl3, jnp.concatenate([hi, zpad, mid, zpad, lo, zpad], axis=0),
                               preferred_element_type=F32))
        lasts = [jnp.where(fwd, lg[c_len - 1:c_len, :], lg[0:1, :]) for lg in lgs]
        kr3s = [_lhs3(jnp.concatenate([kk_ref[rows, sl] * jnp.exp(lg - lw), r_ref[rows, sl] * jnp.exp(lg)],
                                      axis=0), 1)
                for sl, lg, lw in zip(lanes_of, lgs, lws)]
        ens = [jnp.exp(-lg) for lg in lgs]
        gabs = [_dot3_nt(kr3, jnp.concatenate([_rhs3(two_heads(kd_ref[rows, sl] * en), 1),
                                               _rhs3(two_heads(b_ref[rows, sl] * en), 1)], axis=0))
                for kr3, sl, en in zip(kr3s, lanes_of, ens)]
        gas = [gab[:, :LANES] for gab in gabs]
        gbs = [gab[:, LANES:] for gab in gabs]
        ps = [-two_heads(jnp.where(strict, gb[:c_len], 0.0)) for gb in gbs]
        accs = [eye_f + p for p in ps]
        ps = [_mm1(p, p) for p in ps]
        span = 4
        while span < c_len:
            both = [_mm1(jnp.concatenate([p, acc], axis=0), p) for p, acc in zip(ps, accs)]
            accs = [acc + b2[2 * c_len:] for acc, b2 in zip(accs, both)]
            ps = [b2[:2 * c_len] for b2 in both]
            span *= 2
        accs = [acc + _mm1(acc, p) for acc, p in zip(accs, ps)]
        gvs = [_dot3(_lhs3(jnp.concatenate([jnp.where(strict, ga[:c_len], 0.0),
                                            jnp.where(incl, ga[c_len:], 0.0)], axis=0), 1),
                     _rhs3(two_heads(v_ref[rows, sl]), 0))
               for ga, sl in zip(gas, lanes_of)]
        for j in tiles:
            q = ci * nlt + j
            ec = jnp.exp(lasts[j] - lgs[j])
            kr_s[q] = kr3s[j]
            t_s[q] = _lhs3(accs[j][:c_len] + accs[j][c_len:], 1)
            g4_s[q] = _lhs3(jnp.where(incl, gbs[j][c_len:], 0.0), 1)
            gv_s[q] = gvs[j]
            kb_s[q] = _lhs3(jnp.concatenate([kd_ref[rows, lanes_of[j]] * ec, b_ref[rows, lanes_of[j]] * ec],
                                            axis=0), 0)
            gcol = jnp.transpose(jnp.broadcast_to(jnp.exp(lasts[j]), (2 * c_len, LANES)))
            gc_s[q] = jnp.where(bdmask, gcol, 0.0)

    for ci in range(n_ch):
        rows = chunk_rows(ci)
        qs = [ci * nlt + j for j in tiles]
        bdss = [s_ref[j] for j in tiles]
        xrs = [_dot3(kr_s[q], _rhs3(bds, 0)) for q, bds in zip(qs, bdss)]
        us = [_dot3(t_s[q], _rhs3(two_heads(xr[:c_len] + gv_s[q, :c_len]), 0)) for q, xr in zip(qs, xrs)]
        news = [_dot3_tn(kb_s[q], _rhs3(jnp.concatenate([v_ref[rows, sl], -u], axis=0), 0))
                for q, sl, u in zip(qs, lanes_of, us)]
        for j in tiles:
            s_ref[j] = jnp.where(bdmask, news[j], 0.0) + bdss[j] * gc_s[qs[j]]
        for j in tiles:
            y_ref[rows, lanes_of[j]] = (xrs[j][c_len:] + gv_s[qs[j], c_len:]
                                        - _dot3(g4_s[qs[j]], _rhs3(two_heads(us[j]), 0)))

    @pl.when(tstep == nt - 1)
    def _():
        for j in range(nlt):
            bds = s_ref[j]
            sf_ref[:, j * LANES:(j + 1) * LANES] = bds[:HEAD_N] + bds[HEAD_N:]


def wkv_scan(r, kk, v, lw, kd, b, s0, row0, n_seq, s_len, d_r):
    tb = min(256, s_len)
    nt = s_len // tb
    blk0 = row0 // tb
    nlt = 8
    wl = nlt * LANES
    nq = (tb // SCAN_C) * nlt

    def tblk(bi, d, c):
        return blk0 + bi * nt + c + d * (nt - 1 - 2 * c)

    shared = pl.BlockSpec((tb, wl), lambda bi, d, hp, c: (tblk(bi, d, c), hp))
    perdir = pl.BlockSpec((None, tb, wl), lambda bi, d, hp, c: (d, tblk(bi, d, c), hp))
    state = pl.BlockSpec((None, None, HEAD_N, wl), lambda bi, d, hp, c: (bi, d, 0, hp))
    kern = functools.partial(_scan_kernel, tb=tb, nt=nt, nlt=nlt)
    return pl.pallas_call(
        kern,
        grid=(n_seq, 2, d_r // wl, nt),
        in_specs=[shared, shared, shared, perdir, perdir, perdir, state],
        out_specs=[pl.BlockSpec((None, tb, wl),
                                lambda bi, d, hp, c: (d, bi * nt + c + d * (nt - 1 - 2 * c), hp)),
                   state],
        out_shape=[jax.ShapeDtypeStruct((2, n_seq * s_len, d_r), F32),
                   jax.ShapeDtypeStruct((n_seq, 2, HEAD_N, d_r), F32)],
        scratch_shapes=[pltpu.VMEM((nlt, 2 * HEAD_N, LANES), F32),
                        pltpu.VMEM((nq, 2 * SCAN_C, 3 * LANES), BF16),
                        pltpu.VMEM((nq, SCAN_C, 3 * LANES), BF16),
                        pltpu.VMEM((nq, SCAN_C, 3 * LANES), BF16),
                        pltpu.VMEM((nq, 2 * SCAN_C, LANES), F32),
                        pltpu.VMEM((nq, 6 * SCAN_C, LANES), BF16),
                        pltpu.VMEM((nq, 2 * HEAD_N, LANES), F32)],
        compiler_params=_cparams(("arbitrary", "arbitrary", "arbitrary", "arbitrary")),
        name="wkv_scan",
    )(r, kk, v, lw, kd, b, s0)


def _post_kernel(y_ref, bonus_ref, g_ref, pv_ref, bd_ref, o_ref):
    bd3 = bd_ref[...]
    y = y_ref[0] + y_ref[1]
    mu = _segsum(y, bd3) * (1.0 / HEAD_N)
    yc = y - mu
    var = _segsum(yc * yc, bd3) * (1.0 / HEAD_N)
    yn = yc * lax.rsqrt(var + GN_EPS) * pv_ref[3:4, :] + pv_ref[4:5, :]
    o_ref[...] = (yn + bonus_ref[...]) * g_ref[...]


def rwkv_post(y, bonus, g, pv, bd3, l):
    t, d_r = bonus.shape
    tm = 256
    row = pl.BlockSpec((tm, d_r), lambda i: (i, 0))
    return pl.pallas_call(
        _post_kernel,
        grid=(t // tm,),
        in_specs=[pl.BlockSpec((2, tm, d_r), lambda i: (0, i, 0)), row, row,
                  pl.BlockSpec((None, 8, d_r), lambda i: (l, 0, 0)),
                  pl.BlockSpec((3 * LANES, LANES), lambda i: (0, 0))],
        out_specs=row,
        out_shape=jax.ShapeDtypeStruct((t, d_r), F32),
        compiler_params=_cparams(("arbitrary",)),
        name="rwkv_post",
    )(y, bonus, g, pv, bd3)


def _rope(x, cos, sin):
    lane = lax.broadcasted_iota(jnp.int32, (x.shape[0], LANES), 1)
    first = (lane & 31) < 16
    outs = []
    for j in range(x.shape[1] // LANES):
        xc = x[:, j * LANES:(j + 1) * LANES]
        partner = jnp.where(first, pltpu.roll(xc, LANES - 16, axis=1), pltpu.roll(xc, 16, axis=1))
        outs.append(xc * cos + partner * sin)
    return outs[0] if len(outs) == 1 else jnp.concatenate(outs, axis=1)


def _attn_kernel(lam_ref, q_ref, k_ref, v_ref, *rest, use_ctx, n_heads, out_scale):
    if use_ctx:
        (ck_ref, cv_ref, cosq_ref, sinq_ref, cosk_ref, sink_ref, g_ref,
         o_ref, kb_ref, vb_ref, ckb_ref, cvb_ref) = rest
    else:
        g_ref, o_ref, kb_ref, vb_ref = rest
    qi = pl.program_id(1)

    @pl.when(qi == 0)
    def _():
        k = k_ref[...]
        if use_ctx:
            k = _rope(k, cosk_ref[...], sink_ref[...])
            ckb_ref[...] = ck_ref[...].astype(BF16)
            cvb_ref[...] = cv_ref[...].astype(BF16)
        kb_ref[...] = k.astype(BF16)
        vb_ref[...] = v_ref[...].astype(BF16)

    lam = lam_ref[0]
    q = q_ref[...]
    if use_ctx:
        q = _rope(q, cosq_ref[...], sinq_ref[...])
    tq = q.shape[0]
    lane = lax.broadcasted_iota(jnp.int32, (tq, LANES), 1)
    scale = HEAD_N ** -0.5
    dn = (((1,), (1,)), ((), ()))
    for h in range(n_heads):
        sl = slice(h * LANES, (h + 1) * LANES)
        qh = q[:, sl] * scale
        kh = kb_ref[:, sl]
        outs = []
        for comp in range(2):
            qc = jnp.where((lane < HEAD_N) == (comp == 0), qh, 0.0).astype(BF16)
            s_a = lax.dot_general(qc, kh, dn, preferred_element_type=F32)
            m = jnp.max(s_a, axis=-1, keepdims=True)
            if use_ctx:
                s_b = lax.dot_general(qc, ckb_ref[:, sl], dn, preferred_element_type=F32)
                m = jnp.maximum(m, jnp.max(s_b, axis=-1, keepdims=True))
            e_a = jnp.exp(s_a - m)
            den = jnp.sum(e_a, axis=-1, keepdims=True)
            acc = jnp.dot(e_a.astype(BF16), vb_ref[:, sl], preferred_element_type=F32)
            if use_ctx:
                e_b = jnp.exp(s_b - m)
                den = den + jnp.sum(e_b, axis=-1, keepdims=True)
                acc = acc + jnp.dot(e_b.astype(BF16), cvb_ref[:, sl], preferred_element_type=F32)
            outs.append(acc / den)
        o = outs[0] - lam * outs[1]
        o_ref[:, sl] = _rms(o, g_ref[...]) * out_scale


def diff_attn(z, lam, subln3, l, lam_init, row0, n_seq, s_len, d_diff, col0, ctx=None):
    tq = 128
    nq = s_len // tq
    n_heads = d_diff // LANES
    qb0 = row0 // tq
    sb0 = row0 // s_len
    use_ctx = ctx is not None
    in_specs = [pl.BlockSpec(memory_space=pltpu.SMEM),
                pl.BlockSpec((tq, d_diff), lambda bi, qi: (qb0 + bi * nq + qi, col0)),
                pl.BlockSpec((s_len, d_diff), lambda bi, qi: (sb0 + bi, col0 + 1)),
                pl.BlockSpec((s_len, d_diff), lambda bi, qi: (sb0 + bi, col0 + 2))]
    args = [lam, z, z, z]
    scratch = [pltpu.VMEM((s_len, d_diff), BF16), pltpu.VMEM((s_len, d_diff), BF16)]
    if use_ctx:
        cache_k, cache_v, cos, sin = ctx
        p_len = cache_k.shape[2]
        cspec = pl.BlockSpec((None, None, p_len, d_diff), lambda bi, qi: (bi, l, 0, 0))
        in_specs += [cspec, cspec,
                     pl.BlockSpec((tq, LANES), lambda bi, qi: (qi, 0)),
                     pl.BlockSpec((tq, LANES), lambda bi, qi: (qi, 0)),
                     pl.BlockSpec((s_len, LANES), lambda bi, qi: (0, 0)),
                     pl.BlockSpec((s_len, LANES), lambda bi, qi: (0, 0))]
        args += [cache_k, cache_v, cos, sin, cos, sin]
        scratch += [pltpu.VMEM((p_len, d_diff), BF16), pltpu.VMEM((p_len, d_diff), BF16)]
    in_specs.append(pl.BlockSpec((None, 1, LANES), lambda bi, qi: (l, 0, 0)))
    args.append(subln3)
    kern = functools.partial(_attn_kernel, use_ctx=use_ctx, n_heads=n_heads,
                             out_scale=1.0 - lam_init)
    return pl.pallas_call(
        kern,
        grid=(n_seq, nq),
        in_specs=in_specs,
        out_specs=pl.BlockSpec((tq, d_diff), lambda bi, qi: (bi * nq + qi, 0)),
        out_shape=jax.ShapeDtypeStruct((n_seq * s_len, d_diff), F32),
        scratch_shapes=scratch,
        compiler_params=_cparams(("arbitrary", "arbitrary")),
        name="diff_attn_ctx" if use_ctx else "diff_attn",
    )(*args)


def _chunk_kernel(u_ref, g_ref, vn_ref, ws_ref, bs_ref, o_ref):
    u = jax.nn.gelu(u_ref[...], approximate=True)
    vv = _rms(jax.nn.gelu(g_ref[...], approximate=True), vn_ref[...])
    for gi in range(ws_ref.shape[0]):
        sl = slice(gi * LANES, (gi + 1) * LANES)
        s = jnp.dot(ws_ref[gi].astype(BF16), vv[:, sl].astype(BF16), preferred_element_type=F32)
        o_ref[:, sl] = u[:, sl] * (s + bs_ref[:, sl])


def chunk_mix(z, vn3, chunk_ws, bs_exp, l, d_c, col_u):
    t = z.shape[0]
    n_g = chunk_ws.shape[1]
    return pl.pallas_call(
        _chunk_kernel,
        grid=(t // CHUNK,),
        in_specs=[pl.BlockSpec((CHUNK, d_c), lambda i: (i, col_u)),
                  pl.BlockSpec((CHUNK, d_c), lambda i: (i, col_u + 1)),
                  pl.BlockSpec((None, 1, d_c), lambda i: (l, 0, 0)),
                  pl.BlockSpec((None, n_g, CHUNK, CHUNK), lambda i: (l, 0, 0, 0)),
                  pl.BlockSpec((None, CHUNK, d_c), lambda i: (l, 0, 0))],
        out_specs=pl.BlockSpec((CHUNK, d_c), lambda i: (i, 0)),
        out_shape=jax.ShapeDtypeStruct((t, d_c), F32),
        compiler_params=_cparams(("arbitrary",)),
        name="chunk_mix",
    )(z, z, vn3, chunk_ws, bs_exp)


def _outproj_kernel(x_ref, or_ref, od_ref, oc_ref, wr_ref, wd_ref, wc_ref, g1n_ref, g2n_ref,
                    gate_ref, sh_ref, sc_ref, rw_ref, rb_ref, x1_ref, h2_ref, lg_ref):
    mix = jnp.dot(or_ref[...].astype(BF16), wr_ref[...], preferred_element_type=F32)
    mix = mix + jnp.dot(od_ref[...].astype(BF16), wd_ref[...], preferred_element_type=F32)
    mix = mix + jnp.dot(oc_ref[...].astype(BF16), wc_ref[...], preferred_element_type=F32)
    x1 = x_ref[...] + gate_ref[...] * _rms(mix, g1n_ref[...])
    x1_ref[...] = x1
    h2 = _rms(x1, g2n_ref[...]) * (1.0 + sc_ref[...]) + sh_ref[...]
    h2_ref[...] = h2
    hh, hm, hl = _split3(h2)
    wh, wm, wl = _split3(rw_ref[...])
    acc = jnp.dot(hh, wh, preferred_element_type=F32)
    for a, b in ((hh, wm), (hm, wh), (hm, wm), (hh, wl), (hl, wh)):
        acc = acc + jnp.dot(a, b, preferred_element_type=F32)
    lg_ref[...] = acc + rb_ref[...]


def out_proj(x, o_r, o_d, o_c, w_out_b, norm_g3, mods4, rw_p, rb_p, l, t_ctx, s_dec):
    t, d = x.shape
    d_r, d_d, d_c = o_r.shape[1], o_d.shape[1], o_c.shape[1]
    tm = math.gcd(256, t_ctx, s_dec)
    mrow = lambda i: _mod_row(i, tm, t_ctx, s_dec)
    nrm = lambda k: pl.BlockSpec((None, 1, d), lambda i: (4 * l + k, 0, 0))
    mod = lambda k: pl.BlockSpec((None, None, 1, d), lambda i: (mrow(i), k, 0, 0))
    row = lambda w: pl.BlockSpec((tm, w), lambda i: (i, 0))
    nrb = d_r // d_d
    return pl.pallas_call(
        _outproj_kernel,
        grid=(t // tm,),
        in_specs=[row(d), row(d_r), row(d_d), row(d_c),
                  pl.BlockSpec((None, d_r, d), lambda i: (l, 0, 0)),
                  pl.BlockSpec((None, d_d, d), lambda i: (l, nrb, 0)),
                  pl.BlockSpec((None, d_c, d), lambda i: (l, nrb + 1, 0)),
                  nrm(1), nrm(2), mod(2), mod(3), mod(4),
                  pl.BlockSpec((None, d, LANES), lambda i: (l, 0, 0)),
                  pl.BlockSpec((None, 1, LANES), lambda i: (l, 0, 0))],
        out_specs=[row(d), row(d), row(LANES)],
        out_shape=[jax.ShapeDtypeStruct((t, d), F32), jax.ShapeDtypeStruct((t, d), F32),
                   jax.ShapeDtypeStruct((t, LANES), F32)],
        compiler_params=_cparams(("arbitrary",)),
        name="out_proj",
    )(x, o_r, o_d, o_c, w_out_b, w_out_b, w_out_b, norm_g3, norm_g3, mods4, mods4, mods4, rw_p, rb_p)


def _gather_kernel(idx_ref, src_ref, o_ref, sem, *, bm):
    def row_copy(r, src_row):
        return pltpu.make_async_copy(src_ref.at[pl.ds(src_row, 1), :], o_ref.at[pl.ds(r, 1), :], sem)

    def issue(i, carry):
        r = 2 * i
        row_copy(r, idx_ref[0, r]).start(priority=0)
        row_copy(r + 1, idx_ref[0, r + 1]).start(priority=1)
        return carry

    lax.fori_loop(0, bm // 2, issue, 0)
    pltpu.make_async_copy(o_ref, o_ref, sem).wait()


def gather_rows(src, idx, bm=512):
    n, d = src.shape
    m = idx.shape[0]
    return pl.pallas_call(
        functools.partial(_gather_kernel, bm=bm),
        grid=(m // bm,),
        in_specs=[pl.BlockSpec((None, 1, bm), lambda i: (i, 0, 0), memory_space=pltpu.SMEM),
                  pl.BlockSpec(memory_space=pl.ANY)],
        out_specs=pl.BlockSpec((bm, d), lambda i: (i, 0)),
        out_shape=jax.ShapeDtypeStruct((m, d), src.dtype),
        scratch_shapes=[pltpu.SemaphoreType.DMA],
        compiler_params=_cparams(("arbitrary",)),
        name="gather_rows",
    )(idx.reshape(m // bm, 1, bm), src)


def _expert_kernel(be_ref, bv_ref, x_ref, wg_ref, wu_ref, bg_ref, bu_ref, wd_ref, bdn_ref,
                   o_ref, xb_ref, *, nf):
    i = pl.program_id(0)
    f = pl.program_id(1)

    @pl.when(f == 0)
    def _():
        xb_ref[...] = x_ref[...].astype(BF16)
        o_ref[...] = jnp.zeros_like(o_ref)

    @pl.when(bv_ref[i] > 0)
    def _():
        xb = xb_ref[...]
        gt = jnp.dot(xb, wg_ref[...], preferred_element_type=F32) + bg_ref[...]
        up = jnp.dot(xb, wu_ref[...], preferred_element_type=F32) + bu_ref[...]
        gt = jnp.minimum(gt, SWIGLU_LIMIT)
        up = jnp.clip(up, -SWIGLU_LIMIT, SWIGLU_LIMIT)
        act = (up + 1.0) * gt * jax.nn.sigmoid(SWIGLU_ALPHA * gt)
        o_ref[...] += jnp.dot(act.astype(BF16), wd_ref[...], preferred_element_type=F32)

    @pl.when(f == nf - 1)
    def _():
        o_ref[...] = o_ref[...] + bdn_ref[...]


def expert_ffn(xs, block_e, block_valid, w_gu_b, b_gu3, w_dn_b, b_dn3, l, n_e, bm):
    n_slots, d = xs.shape
    ff = w_dn_b.shape[1]
    ft = 512
    nf = ff // ft
    nb = n_slots // bm
    e0 = l * n_e

    def fsel(i, f, bv):
        return jnp.where(bv[i] > 0, f, nf - 1)

    return pl.pallas_call(
        functools.partial(_expert_kernel, nf=nf),
        grid_spec=pltpu.PrefetchScalarGridSpec(
            num_scalar_prefetch=2,
            grid=(nb, nf),
            in_specs=[pl.BlockSpec((bm, d), lambda i, f, be, bv: (i, 0)),
                      pl.BlockSpec((None, d, ft), lambda i, f, be, bv: (e0 + be[i], 0, fsel(i, f, bv))),
                      pl.BlockSpec((None, d, ft), lambda i, f, be, bv: (e0 + be[i], 0, nf + fsel(i, f, bv))),
                      pl.BlockSpec((None, 1, ft), lambda i, f, be, bv: (e0 + be[i], 0, fsel(i, f, bv))),
                      pl.BlockSpec((None, 1, ft), lambda i, f, be, bv: (e0 + be[i], 0, nf + fsel(i, f, bv))),
                      pl.BlockSpec((None, ft, d), lambda i, f, be, bv: (e0 + be[i], fsel(i, f, bv), 0)),
                      pl.BlockSpec((None, 1, d), lambda i, f, be, bv: (e0 + be[i], 0, 0))],
            out_specs=pl.BlockSpec((bm, d), lambda i, f, be, bv: (i, 0)),
            scratch_shapes=[pltpu.VMEM((bm, d), BF16)]),
        out_shape=jax.ShapeDtypeStruct((n_slots, d), F32),
        compiler_params=_cparams(("arbitrary", "arbitrary")),
        name="expert_ffn",
    )(block_e, block_valid, xs, w_gu_b, w_gu_b, b_gu3, b_gu3, w_dn_b, b_dn3)


def _combine_kernel(x_ref, *rest):
    y_refs = rest[:TOP_K]
    rg_ref, gn_ref, gate_ref, o_ref = rest[TOP_K:]
    f = y_refs[0][...] * rg_ref[:, 0:1]
    for k in range(1, TOP_K):
        f = f + y_refs[k][...] * rg_ref[:, k:k + 1]
    o_ref[...] = x_ref[...] + gate_ref[...] * _rms(f, gn_ref[...])


def moe_combine(x1, yk, gates, norm_g3, mods4, l, t_ctx, s_dec):
    t, d = x1.shape
    tm = 256
    nblk = t // tm
    mrow = lambda i: _mod_row(i, tm, t_ctx, s_dec)
    y_specs = [pl.BlockSpec((tm, d), functools.partial(lambda i, k: (k * nblk + i, 0), k=k))
               for k in range(TOP_K)]
    return pl.pallas_call(
        _combine_kernel,
        grid=(nblk,),
        in_specs=[pl.BlockSpec((tm, d), lambda i: (i, 0)),
                  *y_specs,
                  pl.BlockSpec((tm, TOP_K), lambda i: (i, 0)),
                  pl.BlockSpec((None, 1, d), lambda i: (4 * l + 3, 0, 0)),
                  pl.BlockSpec((None, None, 1, d), lambda i: (mrow(i), 5, 0, 0))],
        out_specs=pl.BlockSpec((tm, d), lambda i: (i, 0)),
        out_shape=jax.ShapeDtypeStruct((t, d), F32),
        compiler_params=_cparams(("arbitrary",)),
        name="moe_combine",
    )(x1, *([yk] * TOP_K), gates, norm_g3, mods4)


def _route(logits, n_e, bm):
    t = logits.shape[0]
    top_v, top_i = lax.top_k(logits, TOP_K)
    gates = jax.nn.softmax(top_v, axis=-1)
    tk = t * TOP_K
    flat_e = top_i.reshape(-1).astype(jnp.int32)
    eids = jnp.arange(n_e, dtype=jnp.int32)
    iota = jnp.arange(tk, dtype=jnp.int32)
    sorted_e, order = lax.sort_key_val(flat_e, iota)
    counts = jnp.sum((flat_e[:, None] == eids[None, :]).astype(jnp.int32), axis=0)
    padded = ((counts + bm - 1) // bm) * bm
    pad_end = jnp.cumsum(padded)
    pad_start = pad_end - padded
    start = jnp.cumsum(counts) - counts
    shift = pad_start - start
    dest = iota + jnp.sum(jnp.where(sorted_e[:, None] == eids[None, :], shift[None, :], 0), axis=1)
    _, pos = lax.sort_key_val(order, dest)
    nb = (tk + n_e * (bm - 1) + bm - 1) // bm
    starts = jnp.arange(nb, dtype=jnp.int32) * bm
    block_e = jnp.minimum(jnp.searchsorted(pad_end, starts, side='right'), n_e - 1).astype(jnp.int32)
    block_valid = (starts < pad_end[-1]).astype(jnp.int32)
    src = starts[:, None] + jnp.arange(bm, dtype=jnp.int32)[None, :] - shift[block_e][:, None]
    live = jnp.logical_and(src < (start + counts)[block_e][:, None], block_valid[:, None] > 0)
    slot_tok = jnp.where(live, order[jnp.clip(src, 0, tk - 1)] // TOP_K, 0).reshape(-1)
    last_e = jnp.max(jnp.where(counts > 0, eids, 0))
    block_e = jnp.where(block_valid > 0, block_e, last_e)
    return slot_tok, gates, block_e, block_valid, pos


def _rope_tables(s_len):
    rows = s_len // GRID_W
    row = jnp.repeat(jnp.arange(rows), GRID_W)
    col = jnp.tile(jnp.arange(GRID_W), rows)
    half = HEAD_N // 2
    inv_freq = 1.0 / (ROPE_THETA ** (jnp.arange(0, half, 2, dtype=F32) / half))
    ang = jnp.stack([row, col], axis=-1).astype(F32)[..., None] * inv_freq
    cos, sin = jnp.cos(ang), jnp.sin(ang)
    cos64 = jnp.concatenate([cos[:, 0], cos[:, 0], cos[:, 1], cos[:, 1]], axis=-1)
    sin64 = jnp.concatenate([-sin[:, 0], sin[:, 0], -sin[:, 1], sin[:, 1]], axis=-1)
    return jnp.tile(cos64, (1, 2)), jnp.tile(sin64, (1, 2))


def kernel(x_prompt, x_sample, c, cache_k, cache_v, state_wkv, c_ctx, ada_w, ada_b, norm_g, w_in, shift_mu, decay_w0, decay_w2, iclr_a0, iclr_a2, gate_g2, k_k, k_a, r_k, ln_x_w, ln_x_b, diff_lam, diff_subln, chunk_vnorm, chunk_ws, chunk_bs, w_out, router_w, router_b, w_gu, b_gu, w_dn, b_dn):
    b_ctx, s_ctx, d = x_prompt.shape
    b_dec, s_dec, _ = x_sample.shape
    depth = ada_w.shape[0]
    t_ctx, t_dec = b_ctx * s_ctx, b_dec * s_dec
    d_r = k_k.shape[1]
    n_h = d_r // HEAD_N
    d_rin = shift_mu.shape[2]
    d_diff = diff_subln.shape[1] * (cache_k.shape[3])
    n_g = chunk_ws.shape[1]
    d_c = n_g * chunk_ws.shape[2]
    n_e = router_w.shape[2]
    d_lw, d_la, d_lg = decay_w2.shape[2], iclr_a2.shape[2], gate_g2.shape[1]
    p_len = cache_k.shape[2]
    assert d_lw == HEAD_N and d_la == HEAD_N and d_lg <= 2 * LANES and d_diff == d_c
    wz = 3 * d_r + 4 * LANES
    col_q = wz // d_diff
    assert wz % d_diff == 0 and d_rin <= wz

    padc = wz - d_rin
    w_in_p = jnp.concatenate([w_in[:, :, :d_rin], jnp.zeros((depth, d, padc), F32), w_in[:, :, d_rin:]],
                             axis=2).astype(BF16)
    mu_p = jnp.pad(shift_mu, ((0, 0), (0, 6), (0, padc)))
    pv = jnp.stack([k_k, k_a, r_k.reshape(depth, d_r), ln_x_w, ln_x_b], axis=1)
    pv = jnp.pad(pv, ((0, 0), (0, 3), (0, 0)))
    p2 = jnp.stack([decay_w0.reshape(depth, 2 * d_r), iclr_a0.reshape(depth, 2 * d_r)], axis=1)
    p2 = jnp.pad(p2, ((0, 0), (0, 6), (0, 0)))
    zl = jnp.zeros((depth, HEAD_N, d_r), F32)
    w2bd = jnp.concatenate([jnp.concatenate([decay_w2[:, 0], zl], axis=2),
                            jnp.concatenate([zl, decay_w2[:, 1]], axis=2)], axis=1).astype(BF16)
    a2bd = jnp.concatenate([jnp.concatenate([iclr_a2[:, 0], zl], axis=2),
                            jnp.concatenate([zl, iclr_a2[:, 1]], axis=2)], axis=1).astype(BF16)
    g2p = jnp.pad(gate_g2, ((0, 0), (0, 2 * LANES - d_lg), (0, 0))).astype(BF16)
    lane = jnp.arange(LANES)
    bd = (lane[:, None] // HEAD_N == lane[None, :] // HEAD_N).astype(BF16)
    bd3 = jnp.concatenate([bd, bd, bd], axis=0)
    norm_g3 = norm_g.reshape(depth * 4, 1, d)
    ada_b3 = ada_b.reshape(depth, 1, 6 * d)
    subln3 = diff_subln.reshape(depth, 1, LANES)
    vn3 = chunk_vnorm.reshape(depth, 1, d_c)
    bs_exp = jnp.repeat(jnp.swapaxes(chunk_bs, 1, 2), d_c // n_g, axis=2)
    w_out_b = w_out.astype(BF16)
    rw_p = jnp.pad(router_w, ((0, 0), (0, 0), (0, LANES - n_e)))
    rb_p = jnp.pad(router_b, ((0, 0), (0, LANES - n_e))).reshape(depth, 1, LANES)
    w_gu_b = w_gu.astype(BF16).reshape(depth * n_e, d, -1)
    w_dn_b = w_dn.astype(BF16).reshape(depth * n_e, -1, d)
    b_gu3 = b_gu.reshape(depth * n_e, 1, -1)
    b_dn3 = b_dn.reshape(depth * n_e, 1, d)
    cos_t, sin_t = _rope_tables(s_dec)
    ck4 = cache_k.reshape(b_dec, depth, p_len, d_diff)
    cv4 = cache_v.reshape(b_dec, depth, p_len, d_diff)
    s0_dec = jnp.transpose(state_wkv, (0, 1, 2, 5, 3, 4)).reshape(b_dec, depth, 2, HEAD_N, d_r)
    s0_ctx = jnp.zeros((b_ctx, 2, HEAD_N, d_r), F32)

    cond = jnp.concatenate([c_ctx[None, :], c], axis=0)
    m_rows = -(-cond.shape[0] // 8) * 8
    cond = jnp.pad(cond, ((0, m_rows - cond.shape[0]), (0, 0)))

    x = jnp.concatenate([x_prompt.reshape(t_ctx, d), x_sample.reshape(t_dec, d)], axis=0)
    bm = 512
    ks, vs, ss = [], [], []
    for l in range(depth):
        mods4 = ada_mods(cond, ada_w, ada_b3, l).reshape(m_rows, 6, 1, d)
        z = in_proj(x, mods4, norm_g3, w_in_p, l, t_ctx, s_dec)
        ks.append(z[:t_ctx, (col_q + 1) * d_diff:(col_q + 2) * d_diff])
        vs.append(z[:t_ctx, (col_q + 2) * d_diff:(col_q + 3) * d_diff])

        r, v, kk, w, kd, b, bonus, g = rwkv_prep(z, mu_p, pv, p2, w2bd, a2bd, g2p, bd3, l,
                                                 t_ctx, s_ctx, s_dec, d_r)
        y_c, sf_c = wkv_scan(r, kk, v, w, kd, b, s0_ctx, 0, b_ctx, s_ctx, d_r)
        y_d, _ = wkv_scan(r, kk, v, w, kd, b, s0_dec[:, l], t_ctx, b_dec, s_dec, d_r)
        ss.append(sf_c)
        o_r = rwkv_post(jnp.concatenate([y_c, y_d], axis=1), bonus, g, pv, bd3, l)

        lam_init = 0.8 - 0.6 * math.exp(-0.3 * l)
        lq1, lk1, lq2, lk2 = diff_lam[l].astype(F32)
        lam = (jnp.exp(jnp.sum(lq1 * lk1)) - jnp.exp(jnp.sum(lq2 * lk2)) + lam_init).reshape(1)
        od_c = diff_attn(z, lam, subln3, l, lam_init, 0, b_ctx, s_ctx, d_diff, col_q)
        od_d = diff_attn(z, lam, subln3, l, lam_init, t_ctx, b_dec, s_dec, d_diff, col_q,
                         ctx=(ck4, cv4, cos_t, sin_t))
        o_d = jnp.concatenate([od_c, od_d], axis=0)
        o_c = chunk_mix(z, vn3, chunk_ws, bs_exp, l, d_c, col_q + 3)

        x1, h2, logits = out_proj(x, o_r, o_d, o_c, w_out_b, norm_g3, mods4, rw_p, rb_p, l, t_ctx, s_dec)
        slot_tok, gates, block_e, block_valid, pos = _route(logits[:, :n_e], n_e, bm)
        xs = gather_rows(h2, slot_tok)
        ys = expert_ffn(xs, block_e, block_valid, w_gu_b, b_gu3, w_dn_b, b_dn3, l, n_e, bm)
        yk = gather_rows(ys, pos.reshape(-1, TOP_K).T.reshape(-1))
        x = moe_combine(x1, yk, gates, norm_g3, mods4, l, t_ctx, s_dec)

    n_hd = cache_k.shape[3]
    new_k = jnp.stack(ks, axis=1).reshape(b_ctx, s_ctx, depth, n_hd, 2, HEAD_N)
    new_k = jnp.transpose(new_k, (0, 2, 1, 3, 4, 5))
    new_v = jnp.stack(vs, axis=1).reshape(b_ctx, s_ctx, depth, n_hd, 2 * HEAD_N)
    new_v = jnp.transpose(new_v, (0, 2, 1, 3, 4))
    st = jnp.stack(ss, axis=1).reshape(b_ctx, depth, 2, HEAD_N, n_h, HEAD_N)
    new_s = jnp.transpose(st, (0, 1, 2, 4, 5, 3))
    y_prompt = x[:t_ctx].reshape(b_ctx, s_ctx, d)
    y_sample = x[t_ctx:].reshape(b_dec, s_dec, d)
    return (y_prompt, y_sample, new_k, new_v, new_s)
```

```python
import functools
import math

import jax
import jax.numpy as jnp
from jax import lax
from jax.experimental import pallas as pl
from jax.experimental.pallas import tpu as pltpu

F32 = jnp.float32
BF16 = jnp.bfloat16

LANES = 128
HEAD_N = 64
NORM_EPS = 1e-6
GN_EPS = 64e-5
ROPE_THETA = 10000.0
GRID_W = 64
TOP_K = 4
SWIGLU_LIMIT = 7.0
SWIGLU_ALPHA = 1.702
CHUNK = 128
VMEM_LIMIT = 56 * 1024 * 1024


def _cparams(sem):
    return pltpu.CompilerParams(dimension_semantics=sem, vmem_limit_bytes=VMEM_LIMIT)


def _split3(x):
    hi = x.astype(BF16)
    r1 = x - hi.astype(F32)
    mid = r1.astype(BF16)
    lo = (r1 - mid.astype(F32)).astype(BF16)
    return hi, mid, lo


def _segsum(x, bd3):
    outs = []
    for j in range(x.shape[1] // LANES):
        hi, mid, lo = _split3(x[:, j * LANES:(j + 1) * LANES])
        lhs = jnp.concatenate([hi, mid, lo], axis=1)
        outs.append(jnp.dot(lhs, bd3, preferred_element_type=F32))
    return outs[0] if len(outs) == 1 else jnp.concatenate(outs, axis=1)


def _rms(x, g):
    return x * lax.rsqrt(jnp.mean(x * x, axis=-1, keepdims=True) + NORM_EPS) * g


def _ada_kernel(c_ref, w_ref, b_ref, o_ref):
    c = c_ref[...]
    s = c * jax.nn.sigmoid(c)
    o_ref[...] = jnp.dot(s.astype(BF16), w_ref[...].astype(BF16),
                         preferred_element_type=F32) + b_ref[...]


def ada_mods(cond, ada_w, ada_b3, l):
    m, d = cond.shape
    n = ada_w.shape[2]
    tn = 1024
    return pl.pallas_call(
        _ada_kernel,
        grid=(n // tn,),
        in_specs=[pl.BlockSpec((m, d), lambda j: (0, 0)),
                  pl.BlockSpec((None, d, tn), lambda j: (l, 0, j)),
                  pl.BlockSpec((None, 1, tn), lambda j: (l, 0, j))],
        out_specs=pl.BlockSpec((m, tn), lambda j: (0, j)),
        out_shape=jax.ShapeDtypeStruct((m, n), F32),
        compiler_params=_cparams(("arbitrary",)),
        name="ada_mods",
    )(cond, ada_w, ada_b3)


def _inproj_kernel(x_ref, g_ref, sh_ref, sc_ref, w_ref, o_ref, h_ref):
    @pl.when(pl.program_id(1) == 0)
    def _():
        h = _rms(x_ref[...], g_ref[...]) * (1.0 + sc_ref[...]) + sh_ref[...]
        h_ref[...] = h.astype(BF16)

    o_ref[...] = jnp.dot(h_ref[...], w_ref[...], preferred_element_type=F32)


def _mod_row(i, tm, t_ctx, s_dec):
    r0 = i * tm
    return jnp.where(r0 < t_ctx, 0, 1 + (r0 - t_ctx) // s_dec)


def in_proj(x, mods4, norm_g3, w_in_p, l, t_ctx, s_dec):
    t, d = x.shape
    n = w_in_p.shape[2]
    tm, tn = math.gcd(1024, t_ctx, s_dec), 512
    mrow = lambda i: _mod_row(i, tm, t_ctx, s_dec)
    return pl.pallas_call(
        _inproj_kernel,
        grid=(t // tm, n // tn),
        in_specs=[pl.BlockSpec((tm, d), lambda i, j: (i, 0)),
                  pl.BlockSpec((None, 1, d), lambda i, j: (4 * l, 0, 0)),
                  pl.BlockSpec((None, None, 1, d), lambda i, j: (mrow(i), 0, 0, 0)),
                  pl.BlockSpec((None, None, 1, d), lambda i, j: (mrow(i), 1, 0, 0)),
                  pl.BlockSpec((None, d, tn), lambda i, j: (l, 0, j))],
        out_specs=pl.BlockSpec((tm, tn), lambda i, j: (i, j)),
        out_shape=jax.ShapeDtypeStruct((t, n), F32),
        scratch_shapes=[pltpu.VMEM((tm, d), BF16)],
        compiler_params=_cparams(("arbitrary", "arbitrary")),
        name="in_proj",
    )(x, norm_g3, mods4, mods4, w_in_p)


def _prep_kernel(z_ref, zp_ref, zn_ref, mu_ref, pv_ref, p2_ref, w2_ref, a2_ref, g2_ref, bd_ref,
                 r_ref, v_ref, kk_ref, w_ref, kd_ref, b_ref, bonus_ref, g_ref, zs_ref,
                 *, tm, t_ctx, s_ctx, s_dec, d_r):
    i = pl.program_id(0)
    r0 = i * tm
    in_ctx = r0 < t_ctx
    pos0 = jnp.where(in_ctx, r0 % s_ctx, (r0 - t_ctx) % s_dec)
    slen = jnp.where(in_ctx, s_ctx, s_dec)
    is_start = pos0 == 0
    is_end = pos0 + tm == slen
    width = z_ref.shape[1]
    rows = lax.broadcasted_iota(jnp.int32, (tm, LANES), 0)
    for j in range(width // LANES):
        sl = slice(j * LANES, (j + 1) * LANES)
        zc = z_ref[:, sl]
        prev_row = jnp.where(is_start, 0.0, zp_ref[7:8, sl])
        next_row = jnp.where(is_end, 0.0, zn_ref[0:1, sl])
        zprev = jnp.where(rows == 0, prev_row, pltpu.roll(zc, 1, axis=0))
        znext = jnp.where(rows == tm - 1, next_row, pltpu.roll(zc, tm - 1, axis=0))
        zs_ref[:, sl] = zc + mu_ref[0:1, sl] * (zprev - zc) + mu_ref[1:2, sl] * (znext - zc)

    bd3 = bd_ref[...]
    r = zs_ref[:, 0:d_r]
    k = zs_ref[:, d_r:2 * d_r]
    v = zs_ref[:, 2 * d_r:3 * d_r]
    o = 3 * d_r
    cw = zs_ref[:, o:o + LANES]
    ca = zs_ref[:, o + LANES:o + 2 * LANES]
    cg = zs_ref[:, o + 2 * LANES:o + 4 * LANES]
    k_k = pv_ref[0:1, :]
    k_a = pv_ref[1:2, :]
    r_k = pv_ref[2:3, :]

    r_ref[...] = r
    v_ref[...] = v
    g_ref[...] = jnp.dot(jax.nn.sigmoid(cg).astype(BF16), g2_ref[...], preferred_element_type=F32)

    kkr = k * k_k
    nrm = jnp.sqrt(_segsum(kkr * kkr, bd3))
    kk = kkr / jnp.maximum(nrm, 1e-12)
    kk_ref[...] = kk

    lw = p2_ref[0:1, :] + jnp.dot(jnp.tanh(cw).astype(BF16), w2_ref[...], preferred_element_type=F32)
    decay = -math.exp(-0.5) * jax.nn.sigmoid(lw)
    a = jax.nn.sigmoid(p2_ref[1:2, :] + jnp.dot(ca.astype(BF16), a2_ref[...], preferred_element_type=F32))
    kd_sum = None
    for d in range(2):
        a_d = a[:, d * d_r:(d + 1) * d_r]
        kd = k * (1.0 + (a_d - 1.0) * k_a)
        w_ref[d] = decay[:, d * d_r:(d + 1) * d_r]
        kd_ref[d] = kd
        b_ref[d] = kk * a_d
        kd_sum = kd if kd_sum is None else kd_sum + kd
    bonus_ref[...] = _segsum(r * kd_sum * r_k, bd3) * v


def rwkv_prep(z, mu_p, pv, p2, w2bd, a2bd, g2p, bd3, l, t_ctx, s_ctx, s_dec, d_r):
    t = z.shape[0]
    tm = 128
    wz = mu_p.shape[2]
    nb8 = t // 8
    kern = functools.partial(_prep_kernel, tm=tm, t_ctx=t_ctx, s_ctx=s_ctx, s_dec=s_dec, d_r=d_r)
    row = pl.BlockSpec((tm, d_r), lambda i: (i, 0))
    row2 = pl.BlockSpec((2, tm, d_r), lambda i: (0, i, 0))
    o1 = jax.ShapeDtypeStruct((t, d_r), F32)
    o2 = jax.ShapeDtypeStruct((2, t, d_r), F32)
    return pl.pallas_call(
        kern,
        grid=(t // tm,),
        in_specs=[pl.BlockSpec((tm, wz), lambda i: (i, 0)),
                  pl.BlockSpec((8, wz), lambda i: (jnp.maximum(i * (tm // 8) - 1, 0), 0)),
                  pl.BlockSpec((8, wz), lambda i: (jnp.minimum((i + 1) * (tm // 8), nb8 - 1), 0)),
                  pl.BlockSpec((None, 8, wz), lambda i: (l, 0, 0)),
                  pl.BlockSpec((None, 8, d_r), lambda i: (l, 0, 0)),
                  pl.BlockSpec((None, 8, 2 * d_r), lambda i: (l, 0, 0)),
                  pl.BlockSpec((None, LANES, 2 * d_r), lambda i: (l, 0, 0)),
                  pl.BlockSpec((None, LANES, 2 * d_r), lambda i: (l, 0, 0)),
                  pl.BlockSpec((None, 2 * LANES, d_r), lambda i: (l, 0, 0)),
                  pl.BlockSpec((3 * LANES, LANES), lambda i: (0, 0))],
        out_specs=[row, row, row, row2, row2, row2, row, row],
        out_shape=[o1, o1, o1, o2, o2, o2, o1, o1],
        scratch_shapes=[pltpu.VMEM((tm, wz), F32)],
        compiler_params=_cparams(("arbitrary",)),
        name="rwkv_prep",
    )(z, z, z, mu_p, pv, p2, w2bd, a2bd, g2p, bd3)


def _split2(x):
    hi = x.astype(BF16)
    lo = (x - hi.astype(F32)).astype(BF16)
    return hi, lo


def _lhs3(a, axis):
    ah, al = _split2(a)
    return jnp.concatenate([ah, ah, al], axis=axis)


def _rhs3(b, axis):
    bh, bl = _split2(b)
    return jnp.concatenate([bh, bl, bh], axis=axis)


def _dot3(l3, r3):
    return jnp.dot(l3, r3, preferred_element_type=F32)


def _dot3_nt(l3, r3):
    return lax.dot_general(l3, r3, (((1,), (1,)), ((), ())), preferred_element_type=F32)


def _dot3_tn(l3, r3):
    return lax.dot_general(l3, r3, (((0,), (0,)), ((), ())), preferred_element_type=F32)


def _mm1(a, b):
    return jnp.dot(a.astype(BF16), b.astype(BF16), preferred_element_type=F32)


SCAN_C = 64


def _scan_kernel(r_ref, kk_ref, v_ref, lw_ref, kd_ref, b_ref, s0_ref,
                 y_ref, sf_ref, s_ref, kr_s, t_s, g4_s, gv_s, kb_s, gc_s, *, tb, nt, nlt):
    c_len = SCAN_C
    fwd = pl.program_id(1) == 0
    tstep = pl.program_id(3)
    lane = lax.broadcasted_iota(jnp.int32, (c_len, LANES), 1)
    row = lax.broadcasted_iota(jnp.int32, (c_len, LANES), 0)
    colm = lane & (HEAD_N - 1)
    ahead = (row - colm) * jnp.where(fwd, 1, -1)
    strict = ahead > 0
    incl = ahead >= 0
    h0 = lane < HEAD_N
    lane2 = lax.broadcasted_iota(jnp.int32, (2 * c_len, LANES), 1)
    row2 = lax.broadcasted_iota(jnp.int32, (2 * c_len, LANES), 0)
    bdmask = (lane2 < HEAD_N) == (row2 < HEAD_N)
    eye = lane2 == row2
    m_incl = jnp.where(jnp.logical_and(incl, h0), 1.0, 0.0).astype(BF16)
    m_incl3 = jnp.concatenate([m_incl, m_incl, m_incl], axis=1)
    zpad = jnp.zeros((c_len, LANES), BF16)

    def two_heads(x):
        return jnp.concatenate([jnp.where(h0, x, 0.0), jnp.where(h0, 0.0, x)], axis=0)

    @pl.when(tstep == 0)
    def _():
        for j in range(nlt):
            s_ref[j] = two_heads(s0_ref[:, j * LANES:(j + 1) * LANES])

    n_ch = tb // c_len

    def chunk_rows(ci):
        cc = jnp.where(fwd, ci, n_ch - 1 - ci)
        return pl.ds(pl.multiple_of(cc * c_len, c_len), c_len)

    tiles = range(nlt)
    lanes_of = [slice(j * LANES, (j + 1) * LANES) for j in tiles]
    eye_f = jnp.where(eye, 1.0, 0.0)
    for ci in range(n_ch):
        rows = chunk_rows(ci)
        lws = [lw_ref[rows, sl] for sl in lanes_of]
        lgs = []
        for lw in lws:
            hi, mid, lo = _split3(lw)
            lgs.append(jnp.dot(m_incl3, jnp.concatenate([hi, zpad, mid, zpad, lo, zpad], axis=0),
                               preferred_element_type=F32))
        lasts = [jnp.where(fwd, lg[c_len - 1:c_len, :], lg[0:1, :]) for lg in lgs]
        kr3s = [_lhs3(jnp.concatenate([kk_ref[rows, sl] * jnp.exp(lg - lw), r_ref[rows, sl] * jnp.exp(lg)],
                                      axis=0), 1)
                for sl, lg, lw in zip(lanes_of, lgs, lws)]
        ens = [jnp.exp(-lg) for lg in lgs]
        gabs = [_dot3_nt(kr3, jnp.concatenate([_rhs3(two_heads(kd_ref[rows, sl] * en), 1),
                                               _rhs3(two_heads(b_ref[rows, sl] * en), 1)], axis=0))
                for kr3, sl, en in zip(kr3s, lanes_of, ens)]
        gas = [gab[:, :LANES] for gab in gabs]
        gbs = [gab[:, LANES:] for gab in gabs]
        ps = [-two_heads(jnp.where(strict, gb[:c_len], 0.0)) for gb in gbs]
        accs = [eye_f + p for p in ps]
        ps = [_mm1(p, p) for p in ps]
        span = 4
        while span < c_len:
            both = [_mm1(jnp.concatenate([p, acc], axis=0), p) for p, acc in zip(ps, accs)]
            accs = [acc + b2[2 * c_len:] for acc, b2 in zip(accs, both)]
            ps = [b2[:2 * c_len] for b2 in both]
            span *= 2
        accs = [acc + _mm1(acc, p) for acc, p in zip(accs, ps)]
        gvs = [_dot3(_lhs3(jnp.concatenate([jnp.where(strict, ga[:c_len], 0.0),
                                            jnp.where(incl, ga[c_len:], 0.0)], axis=0), 1),
                     _rhs3(two_heads(v_ref[rows, sl]), 0))
               for ga, sl in zip(gas, lanes_of)]
        for j in tiles:
            q = ci * nlt + j
            ec = jnp.exp(lasts[j] - lgs[j])
            kr_s[q] = kr3s[j]
            t_s[q] = _lhs3(accs[j][:c_len] + accs[j][c_len:], 1)
            g4_s[q] = _lhs3(jnp.where(incl, gbs[j][c_len:], 0.0), 1)
            gv_s[q] = gvs[j]
            kb_s[q] = _lhs3(jnp.concatenate([kd_ref[rows, lanes_of[j]] * ec, b_ref[rows, lanes_of[j]] * ec],
                                            axis=0), 0)
            gcol = jnp.transpose(jnp.broadcast_to(jnp.exp(lasts[j]), (2 * c_len, LANES)))
            gc_s[q] = jnp.where(bdmask, gcol, 0.0)

    for ci in range(n_ch):
        rows = chunk_rows(ci)
        qs = [ci * nlt + j for j in tiles]
        bdss = [s_ref[j] for j in tiles]
        xrs = [_dot3(kr_s[q], _rhs3(bds, 0)) for q, bds in zip(qs, bdss)]
        us = [_dot3(t_s[q], _rhs3(two_heads(xr[:c_len] + gv_s[q, :c_len]), 0)) for q, xr in zip(qs, xrs)]
        news = [_dot3_tn(kb_s[q], _rhs3(jnp.concatenate([v_ref[rows, sl], -u], axis=0), 0))
                for q, sl, u in zip(qs, lanes_of, us)]
        for j in tiles:
            s_ref[j] = jnp.where(bdmask, news[j], 0.0) + bdss[j] * gc_s[qs[j]]
        for j in tiles:
            y_ref[rows, lanes_of[j]] = (xrs[j][c_len:] + gv_s[qs[j], c_len:]
                                        - _dot3(g4_s[qs[j]], _rhs3(two_heads(us[j]), 0)))

    @pl.when(tstep == nt - 1)
    def _():
        for j in range(nlt):
            bds = s_ref[j]
            sf_ref[:, j * LANES:(j + 1) * LANES] = bds[:HEAD_N] + bds[HEAD_N:]


def wkv_scan(r, kk, v, lw, kd, b, s0, row0, n_seq, s_len, d_r):
    tb = min(256, s_len)
    nt = s_len // tb
    blk0 = row0 // tb
    nlt = 8
    wl = nlt * LANES
    nq = (tb // SCAN_C) * nlt

    def tblk(bi, d, c):
        return blk0 + bi * nt + c + d * (nt - 1 - 2 * c)

    shared = pl.BlockSpec((tb, wl), lambda bi, d, hp, c: (tblk(bi, d, c), hp))
    perdir = pl.BlockSpec((None, tb, wl), lambda bi, d, hp, c: (d, tblk(bi, d, c), hp))
    state = pl.BlockSpec((None, None, HEAD_N, wl), lambda bi, d, hp, c: (bi, d, 0, hp))
    kern = functools.partial(_scan_kernel, tb=tb, nt=nt, nlt=nlt)
    return pl.pallas_call(
        kern,
        grid=(n_seq, 2, d_r // wl, nt),
        in_specs=[shared, shared, shared, perdir, perdir, perdir, state],
        out_specs=[pl.BlockSpec((None, tb, wl),
                                lambda bi, d, hp, c: (d, bi * nt + c + d * (nt - 1 - 2 * c), hp)),
                   state],
        out_shape=[jax.ShapeDtypeStruct((2, n_seq * s_len, d_r), F32),
                   jax.ShapeDtypeStruct((n_seq, 2, HEAD_N, d_r), F32)],
        scratch_shapes=[pltpu.VMEM((nlt, 2 * HEAD_N, LANES), F32),
                        pltpu.VMEM((nq, 2 * SCAN_C, 3 * LANES), BF16),
                        pltpu.VMEM((nq, SCAN_C, 3 * LANES), BF16),
                        pltpu.VMEM((nq, SCAN_C, 3 * LANES), BF16),
                        pltpu.VMEM((nq, 2 * SCAN_C, LANES), F32),
                        pltpu.VMEM((nq, 6 * SCAN_C, LANES), BF16),
                        pltpu.VMEM((nq, 2 * HEAD_N, LANES), F32)],
        compiler_params=_cparams(("arbitrary", "arbitrary", "arbitrary", "arbitrary")),
        name="wkv_scan",
    )(r, kk, v, lw, kd, b, s0)


def _post_kernel(y_ref, bonus_ref, g_ref, pv_ref, bd_ref, o_ref):
    bd3 = bd_ref[...]
    y = y_ref[0] + y_ref[1]
    mu = _segsum(y, bd3) * (1.0 / HEAD_N)
    yc = y - mu
    var = _segsum(yc * yc, bd3) * (1.0 / HEAD_N)
    yn = yc * lax.rsqrt(var + GN_EPS) * pv_ref[3:4, :] + pv_ref[4:5, :]
    o_ref[...] = (yn + bonus_ref[...]) * g_ref[...]


def rwkv_post(y, bonus, g, pv, bd3, l):
    t, d_r = bonus.shape
    tm = 256
    row = pl.BlockSpec((tm, d_r), lambda i: (i, 0))
    return pl.pallas_call(
        _post_kernel,
        grid=(t // tm,),
        in_specs=[pl.BlockSpec((2, tm, d_r), lambda i: (0, i, 0)), row, row,
                  pl.BlockSpec((None, 8, d_r), lambda i: (l, 0, 0)),
                  pl.BlockSpec((3 * LANES, LANES), lambda i: (0, 0))],
        out_specs=row,
        out_shape=jax.ShapeDtypeStruct((t, d_r), F32),
        compiler_params=_cparams(("arbitrary",)),
        name="rwkv_post",
    )(y, bonus, g, pv, bd3)


def _rope(x, cos, sin):
    lane = lax.broadcasted_iota(jnp.int32, (x.shape[0], LANES), 1)
    first = (lane & 31) < 16
    outs = []
    for j in range(x.shape[1] // LANES):
        xc = x[:, j * LANES:(j + 1) * LANES]
        partner = jnp.where(first, pltpu.roll(xc, LANES - 16, axis=1), pltpu.roll(xc, 16, axis=1))
        outs.append(xc * cos + partner * sin)
    return outs[0] if len(outs) == 1 else jnp.concatenate(outs, axis=1)


def _attn_kernel(lam_ref, q_ref, k_ref, v_ref, *rest, use_ctx, n_heads, out_scale):
    if use_ctx:
        (ck_ref, cv_ref, cosq_ref, sinq_ref, cosk_ref, sink_ref, g_ref,
         o_ref, kb_ref, vb_ref, ckb_ref, cvb_ref) = rest
    else:
        g_ref, o_ref, kb_ref, vb_ref = rest
    qi = pl.program_id(1)

    @pl.when(qi == 0)
    def _():
        k = k_ref[...]
        if use_ctx:
            k = _rope(k, cosk_ref[...], sink_ref[...])
            ckb_ref[...] = ck_ref[...].astype(BF16)
            cvb_ref[...] = cv_ref[...].astype(BF16)
        kb_ref[...] = k.astype(BF16)
        vb_ref[...] = v_ref[...].astype(BF16)

    lam = lam_ref[0]
    q = q_ref[...]
    if use_ctx:
        q = _rope(q, cosq_ref[...], sinq_ref[...])
    tq = q.shape[0]
    lane = lax.broadcasted_iota(jnp.int32, (tq, LANES), 1)
    scale = HEAD_N ** -0.5
    dn = (((1,), (1,)), ((), ()))
    for h in range(n_heads):
        sl = slice(h * LANES, (h + 1) * LANES)
        qh = q[:, sl] * scale
        kh = kb_ref[:, sl]
        outs = []
        for comp in range(2):
            qc = jnp.where((lane < HEAD_N) == (comp == 0), qh, 0.0).astype(BF16)
            s_a = lax.dot_general(qc, kh, dn, preferred_element_type=F32)
            m = jnp.max(s_a, axis=-1, keepdims=True)
            if use_ctx:
                s_b = lax.dot_general(qc, ckb_ref[:, sl], dn, preferred_element_type=F32)
                m = jnp.maximum(m, jnp.max(s_b, axis=-1, keepdims=True))
            e_a = jnp.exp(s_a - m)
            den = jnp.sum(e_a, axis=-1, keepdims=True)
            acc = jnp.dot(e_a.astype(BF16), vb_ref[:, sl], preferred_element_type=F32)
            if use_ctx:
                e_b = jnp.exp(s_b - m)
                den = den + jnp.sum(e_b, axis=-1, keepdims=True)
                acc = acc + jnp.dot(e_b.astype(BF16), cvb_ref[:, sl], preferred_element_type=F32)
            outs.append(acc / den)
        o = outs[0] - lam * outs[1]
        o_ref[:, sl] = _rms(o, g_ref[...]) * out_scale


def diff_attn(z, lam, subln3, l, lam_init, row0, n_seq, s_len, d_diff, col0, ctx=None):
    tq = 256
    nq = s_len // tq
    n_heads = d_diff // LANES
    qb0 = row0 // tq
    sb0 = row0 // s_len
    use_ctx = ctx is not None
    in_specs = [pl.BlockSpec(memory_space=pltpu.SMEM),
                pl.BlockSpec((tq, d_diff), lambda bi, qi: (qb0 + bi * nq + qi, col0)),
                pl.BlockSpec((s_len, d_diff), lambda bi, qi: (sb0 + bi, col0 + 1)),
                pl.BlockSpec((s_len, d_diff), lambda bi, qi: (sb0 + bi, col0 + 2))]
    args = [lam, z, z, z]
    scratch = [pltpu.VMEM((s_len, d_diff), BF16), pltpu.VMEM((s_len, d_diff), BF16)]
    if use_ctx:
        cache_k, cache_v, cos, sin = ctx
        p_len = cache_k.shape[2]
        cspec = pl.BlockSpec((None, None, p_len, d_diff), lambda bi, qi: (bi, l, 0, 0))
        in_specs += [cspec, cspec,
                     pl.BlockSpec((tq, LANES), lambda bi, qi: (qi, 0)),
                     pl.BlockSpec((tq, LANES), lambda bi, qi: (qi, 0)),
                     pl.BlockSpec((s_len, LANES), lambda bi, qi: (0, 0)),
                     pl.BlockSpec((s_len, LANES), lambda bi, qi: (0, 0))]
        args += [cache_k, cache_v, cos, sin, cos, sin]
        scratch += [pltpu.VMEM((p_len, d_diff), BF16), pltpu.VMEM((p_len, d_diff), BF16)]
    in_specs.append(pl.BlockSpec((None, 1, LANES), lambda bi, qi: (l, 0, 0)))
    args.append(subln3)
    kern = functools.partial(_attn_kernel, use_ctx=use_ctx, n_heads=n_heads,
                             out_scale=1.0 - lam_init)
    return pl.pallas_call(
        kern,
        grid=(n_seq, nq),
        in_specs=in_specs,
        out_specs=pl.BlockSpec((tq, d_diff), lambda bi, qi: (bi * nq + qi, 0)),
        out_shape=jax.ShapeDtypeStruct((n_seq * s_len, d_diff), F32),
        scratch_shapes=scratch,
        compiler_params=_cparams(("arbitrary", "arbitrary")),
        name="diff_attn_ctx" if use_ctx else "diff_attn",
    )(*args)


def _chunk_kernel(u_ref, g_ref, vn_ref, ws_ref, bs_ref, o_ref):
    u = jax.nn.gelu(u_ref[...], approximate=True)
    vv = _rms(jax.nn.gelu(g_ref[...], approximate=True), vn_ref[...])
    for gi in range(ws_ref.shape[0]):
        sl = slice(gi * LANES, (gi + 1) * LANES)
        s = jnp.dot(ws_ref[gi].astype(BF16), vv[:, sl].astype(BF16), preferred_element_type=F32)
        o_ref[:, sl] = u[:, sl] * (s + bs_ref[:, sl])


def chunk_mix(z, vn3, chunk_ws, bs_exp, l, d_c, col_u):
    t = z.shape[0]
    n_g = chunk_ws.shape[1]
    return pl.pallas_call(
        _chunk_kernel,
        grid=(t // CHUNK,),
        in_specs=[pl.BlockSpec((CHUNK, d_c), lambda i: (i, col_u)),
                  pl.BlockSpec((CHUNK, d_c), lambda i: (i, col_u + 1)),
                  pl.BlockSpec((None, 1, d_c), lambda i: (l, 0, 0)),
                  pl.BlockSpec((None, n_g, CHUNK, CHUNK), lambda i: (l, 0, 0, 0)),
                  pl.BlockSpec((None, CHUNK, d_c), lambda i: (l, 0, 0))],
        out_specs=pl.BlockSpec((CHUNK, d_c), lambda i: (i, 0)),
        out_shape=jax.ShapeDtypeStruct((t, d_c), F32),
        compiler_params=_cparams(("arbitrary",)),
        name="chunk_mix",
    )(z, z, vn3, chunk_ws, bs_exp)


def _outproj_kernel(x_ref, or_ref, od_ref, oc_ref, wr_ref, wd_ref, wc_ref, g1n_ref, g2n_ref,
                    gate_ref, sh_ref, sc_ref, rw_ref, rb_ref, x1_ref, h2_ref, lg_ref):
    mix = jnp.dot(or_ref[...].astype(BF16), wr_ref[...], preferred_element_type=F32)
    mix = mix + jnp.dot(od_ref[...].astype(BF16), wd_ref[...], preferred_element_type=F32)
    mix = mix + jnp.dot(oc_ref[...].astype(BF16), wc_ref[...], preferred_element_type=F32)
    x1 = x_ref[...] + gate_ref[...] * _rms(mix, g1n_ref[...])
    x1_ref[...] = x1
    h2 = _rms(x1, g2n_ref[...]) * (1.0 + sc_ref[...]) + sh_ref[...]
    half = h2.shape[1] // 2
    bits = lax.bitcast_convert_type(h2.astype(BF16).astype(F32), jnp.uint32)
    h2_ref[...] = (bits[:, :half] >> 16) | bits[:, half:]
    lg_ref[...] = _dot3(_lhs3(h2, 1), _rhs3(rw_ref[...], 0)) + rb_ref[...]


def out_proj(x, o_r, o_d, o_c, w_out_b, norm_g3, mods4, rw_p, rb_p, l, t_ctx, s_dec):
    t, d = x.shape
    d_r, d_d, d_c = o_r.shape[1], o_d.shape[1], o_c.shape[1]
    tm = math.gcd(256, t_ctx, s_dec)
    mrow = lambda i: _mod_row(i, tm, t_ctx, s_dec)
    nrm = lambda k: pl.BlockSpec((None, 1, d), lambda i: (4 * l + k, 0, 0))
    mod = lambda k: pl.BlockSpec((None, None, 1, d), lambda i: (mrow(i), k, 0, 0))
    row = lambda w: pl.BlockSpec((tm, w), lambda i: (i, 0))
    nrb = d_r // d_d
    return pl.pallas_call(
        _outproj_kernel,
        grid=(t // tm,),
        in_specs=[row(d), row(d_r), row(d_d), row(d_c),
                  pl.BlockSpec((None, d_r, d), lambda i: (l, 0, 0)),
                  pl.BlockSpec((None, d_d, d), lambda i: (l, nrb, 0)),
                  pl.BlockSpec((None, d_c, d), lambda i: (l, nrb + 1, 0)),
                  nrm(1), nrm(2), mod(2), mod(3), mod(4),
                  pl.BlockSpec((None, d, LANES), lambda i: (l, 0, 0)),
                  pl.BlockSpec((None, 1, LANES), lambda i: (l, 0, 0))],
        out_specs=[row(d), row(d // 2), row(LANES)],
        out_shape=[jax.ShapeDtypeStruct((t, d), F32), jax.ShapeDtypeStruct((t, d // 2), jnp.uint32),
                   jax.ShapeDtypeStruct((t, LANES), F32)],
        compiler_params=_cparams(("arbitrary",)),
        name="out_proj",
    )(x, o_r, o_d, o_c, w_out_b, w_out_b, w_out_b, norm_g3, norm_g3, mods4, mods4, mods4, rw_p, rb_p)


def _gather_kernel(idx_ref, src_ref, o_ref, sem, *, bm):
    def row_copy(r, src_row):
        return pltpu.make_async_copy(src_ref.at[pl.ds(src_row, 1), :], o_ref.at[pl.ds(r, 1), :], sem)

    def issue(i, carry):
        r = 2 * i
        row_copy(r, idx_ref[0, r]).start(priority=0)
        row_copy(r + 1, idx_ref[0, r + 1]).start(priority=1)
        return carry

    lax.fori_loop(0, bm // 2, issue, 0)
    pltpu.make_async_copy(o_ref, o_ref, sem).wait()


def gather_rows(src, idx, bm=512):
    n, d = src.shape
    m = idx.shape[0]
    return pl.pallas_call(
        functools.partial(_gather_kernel, bm=bm),
        grid=(m // bm,),
        in_specs=[pl.BlockSpec((None, 1, bm), lambda i: (i, 0, 0), memory_space=pltpu.SMEM),
                  pl.BlockSpec(memory_space=pl.ANY)],
        out_specs=pl.BlockSpec((bm, d), lambda i: (i, 0)),
        out_shape=jax.ShapeDtypeStruct((m, d), src.dtype),
        scratch_shapes=[pltpu.SemaphoreType.DMA],
        compiler_params=_cparams(("arbitrary",)),
        name="gather_rows",
    )(idx.reshape(m // bm, 1, bm), src)


def _expert_kernel(be_ref, bv_ref, x_ref, wg_ref, wu_ref, bg_ref, bu_ref, wd_ref, bdn_ref,
                   o_ref, xb_ref, *, nf):
    i = pl.program_id(0)
    f = pl.program_id(1)

    @pl.when(f == 0)
    def _():
        p = x_ref[...]
        half = p.shape[1]
        xb_ref[:, :half] = lax.bitcast_convert_type(p << 16, F32).astype(BF16)
        xb_ref[:, half:] = lax.bitcast_convert_type(p & jnp.uint32(0xFFFF0000), F32).astype(BF16)
        o_ref[...] = jnp.zeros_like(o_ref)

    @pl.when(bv_ref[i] > 0)
    def _():
        xb = xb_ref[...]
        gt = jnp.dot(xb, wg_ref[...], preferred_element_type=F32) + bg_ref[...]
        up = jnp.dot(xb, wu_ref[...], preferred_element_type=F32) + bu_ref[...]
        gt = jnp.minimum(gt, SWIGLU_LIMIT)
        up = jnp.clip(up, -SWIGLU_LIMIT, SWIGLU_LIMIT)
        act = (up + 1.0) * gt * jax.nn.sigmoid(SWIGLU_ALPHA * gt)
        o_ref[...] += jnp.dot(act.astype(BF16), wd_ref[...], preferred_element_type=F32)

    @pl.when(f == nf - 1)
    def _():
        o_ref[...] = o_ref[...] + bdn_ref[...]


def expert_ffn(xs, block_e, block_valid, w_gu_b, b_gu3, w_dn_b, b_dn3, l, n_e, bm):
    n_slots, d_half = xs.shape
    d = 2 * d_half
    ff = w_dn_b.shape[1]
    ft = 512
    nf = ff // ft
    nb = n_slots // bm
    e0 = l * n_e

    def fsel(i, f, bv):
        return jnp.where(bv[i] > 0, f, nf - 1)

    return pl.pallas_call(
        functools.partial(_expert_kernel, nf=nf),
        grid_spec=pltpu.PrefetchScalarGridSpec(
            num_scalar_prefetch=2,
            grid=(nb, nf),
            in_specs=[pl.BlockSpec((bm, d_half), lambda i, f, be, bv: (i, 0)),
                      pl.BlockSpec((None, d, ft), lambda i, f, be, bv: (e0 + be[i], 0, fsel(i, f, bv))),
                      pl.BlockSpec((None, d, ft), lambda i, f, be, bv: (e0 + be[i], 0, nf + fsel(i, f, bv))),
                      pl.BlockSpec((None, 1, ft), lambda i, f, be, bv: (e0 + be[i], 0, fsel(i, f, bv))),
                      pl.BlockSpec((None, 1, ft), lambda i, f, be, bv: (e0 + be[i], 0, nf + fsel(i, f, bv))),
                      pl.BlockSpec((None, ft, d), lambda i, f, be, bv: (e0 + be[i], fsel(i, f, bv), 0)),
                      pl.BlockSpec((None, 1, d), lambda i, f, be, bv: (e0 + be[i], 0, 0))],
            out_specs=pl.BlockSpec((bm, d), lambda i, f, be, bv: (i, 0)),
            scratch_shapes=[pltpu.VMEM((bm, d), BF16)]),
        out_shape=jax.ShapeDtypeStruct((n_slots, d), F32),
        compiler_params=_cparams(("arbitrary", "arbitrary")),
        name="expert_ffn",
    )(block_e, block_valid, xs, w_gu_b, w_gu_b, b_gu3, b_gu3, w_dn_b, b_dn3)


def _combine_kernel(x_ref, *rest):
    y_refs = rest[:TOP_K]
    rg_ref, gn_ref, gate_ref, o_ref = rest[TOP_K:]
    f = y_refs[0][...] * rg_ref[:, 0:1]
    for k in range(1, TOP_K):
        f = f + y_refs[k][...] * rg_ref[:, k:k + 1]
    o_ref[...] = x_ref[...] + gate_ref[...] * _rms(f, gn_ref[...])


def moe_combine(x1, yk, gates, norm_g3, mods4, l, t_ctx, s_dec):
    t, d = x1.shape
    tm = 256
    nblk = t // tm
    mrow = lambda i: _mod_row(i, tm, t_ctx, s_dec)
    y_specs = [pl.BlockSpec((tm, d), functools.partial(lambda i, k: (k * nblk + i, 0), k=k))
               for k in range(TOP_K)]
    return pl.pallas_call(
        _combine_kernel,
        grid=(nblk,),
        in_specs=[pl.BlockSpec((tm, d), lambda i: (i, 0)),
                  *y_specs,
                  pl.BlockSpec((tm, TOP_K), lambda i: (i, 0)),
                  pl.BlockSpec((None, 1, d), lambda i: (4 * l + 3, 0, 0)),
                  pl.BlockSpec((None, None, 1, d), lambda i: (mrow(i), 5, 0, 0))],
        out_specs=pl.BlockSpec((tm, d), lambda i: (i, 0)),
        out_shape=jax.ShapeDtypeStruct((t, d), F32),
        compiler_params=_cparams(("arbitrary",)),
        name="moe_combine",
    )(x1, *([yk] * TOP_K), gates, norm_g3, mods4)


def _route(logits, n_e, bm):
    t = logits.shape[0]
    top_v, top_i = lax.top_k(logits, TOP_K)
    gates = jax.nn.softmax(top_v, axis=-1)
    tk = t * TOP_K
    flat_e = top_i.reshape(-1).astype(jnp.int32)
    eids = jnp.arange(n_e, dtype=jnp.int32)
    iota = jnp.arange(tk, dtype=jnp.int32)
    sorted_e, order = lax.sort_key_val(flat_e, iota)
    counts = jnp.sum((flat_e[:, None] == eids[None, :]).astype(jnp.int32), axis=0)
    padded = ((counts + bm - 1) // bm) * bm
    pad_end = jnp.cumsum(padded)
    pad_start = pad_end - padded
    start = jnp.cumsum(counts) - counts
    shift = pad_start - start
    dest = iota + jnp.sum(jnp.where(sorted_e[:, None] == eids[None, :], shift[None, :], 0), axis=1)
    _, pos = lax.sort_key_val(order, dest)
    nb = (tk + n_e * (bm - 1) + bm - 1) // bm
    starts = jnp.arange(nb, dtype=jnp.int32) * bm
    block_e = jnp.minimum(jnp.sum((pad_end[None, :] <= starts[:, None]).astype(jnp.int32), axis=1), n_e - 1)
    block_valid = (starts < pad_end[-1]).astype(jnp.int32)
    src = starts[:, None] + jnp.arange(bm, dtype=jnp.int32)[None, :] - shift[block_e][:, None]
    live = jnp.logical_and(src < (start + counts)[block_e][:, None], block_valid[:, None] > 0)
    slot_tok = jnp.where(live, order[jnp.clip(src, 0, tk - 1)] // TOP_K, 0).reshape(-1)
    last_e = jnp.max(jnp.where(counts > 0, eids, 0))
    block_e = jnp.where(block_valid > 0, block_e, last_e)
    return slot_tok, gates, block_e, block_valid, pos


def _rope_tables(s_len):
    rows = s_len // GRID_W
    row = jnp.repeat(jnp.arange(rows), GRID_W)
    col = jnp.tile(jnp.arange(GRID_W), rows)
    half = HEAD_N // 2
    inv_freq = 1.0 / (ROPE_THETA ** (jnp.arange(0, half, 2, dtype=F32) / half))
    ang = jnp.stack([row, col], axis=-1).astype(F32)[..., None] * inv_freq
    cos, sin = jnp.cos(ang), jnp.sin(ang)
    cos64 = jnp.concatenate([cos[:, 0], cos[:, 0], cos[:, 1], cos[:, 1]], axis=-1)
    sin64 = jnp.concatenate([-sin[:, 0], sin[:, 0], -sin[:, 1], sin[:, 1]], axis=-1)
    return jnp.tile(cos64, (1, 2)), jnp.tile(sin64, (1, 2))


def kernel(x_prompt, x_sample, c, cache_k, cache_v, state_wkv, c_ctx, ada_w, ada_b, norm_g, w_in, shift_mu, decay_w0, decay_w2, iclr_a0, iclr_a2, gate_g2, k_k, k_a, r_k, ln_x_w, ln_x_b, diff_lam, diff_subln, chunk_vnorm, chunk_ws, chunk_bs, w_out, router_w, router_b, w_gu, b_gu, w_dn, b_dn):
    b_ctx, s_ctx, d = x_prompt.shape
    b_dec, s_dec, _ = x_sample.shape
    depth = ada_w.shape[0]
    t_ctx, t_dec = b_ctx * s_ctx, b_dec * s_dec
    d_r = k_k.shape[1]
    n_h = d_r // HEAD_N
    d_rin = shift_mu.shape[2]
    d_diff = diff_subln.shape[1] * (cache_k.shape[3])
    n_g = chunk_ws.shape[1]
    d_c = n_g * chunk_ws.shape[2]
    n_e = router_w.shape[2]
    d_lw, d_la, d_lg = decay_w2.shape[2], iclr_a2.shape[2], gate_g2.shape[1]
    p_len = cache_k.shape[2]
    assert d_lw == HEAD_N and d_la == HEAD_N and d_lg <= 2 * LANES and d_diff == d_c
    wz = 3 * d_r + 4 * LANES
    col_q = wz // d_diff
    assert wz % d_diff == 0 and d_rin <= wz

    padc = wz - d_rin
    w_in_p = jnp.concatenate([w_in[:, :, :d_rin], jnp.zeros((depth, d, padc), F32), w_in[:, :, d_rin:]],
                             axis=2).astype(BF16)
    mu_p = jnp.pad(shift_mu, ((0, 0), (0, 6), (0, padc)))
    pv = jnp.stack([k_k, k_a, r_k.reshape(depth, d_r), ln_x_w, ln_x_b], axis=1)
    pv = jnp.pad(pv, ((0, 0), (0, 3), (0, 0)))
    p2 = jnp.stack([decay_w0.reshape(depth, 2 * d_r), iclr_a0.reshape(depth, 2 * d_r)], axis=1)
    p2 = jnp.pad(p2, ((0, 0), (0, 6), (0, 0)))
    zl = jnp.zeros((depth, HEAD_N, d_r), F32)
    w2bd = jnp.concatenate([jnp.concatenate([decay_w2[:, 0], zl], axis=2),
                            jnp.concatenate([zl, decay_w2[:, 1]], axis=2)], axis=1).astype(BF16)
    a2bd = jnp.concatenate([jnp.concatenate([iclr_a2[:, 0], zl], axis=2),
                            jnp.concatenate([zl, iclr_a2[:, 1]], axis=2)], axis=1).astype(BF16)
    g2p = jnp.pad(gate_g2, ((0, 0), (0, 2 * LANES - d_lg), (0, 0))).astype(BF16)
    lane = jnp.arange(LANES)
    bd = (lane[:, None] // HEAD_N == lane[None, :] // HEAD_N).astype(BF16)
    bd3 = jnp.concatenate([bd, bd, bd], axis=0)
    norm_g3 = norm_g.reshape(depth * 4, 1, d)
    ada_b3 = ada_b.reshape(depth, 1, 6 * d)
    subln3 = diff_subln.reshape(depth, 1, LANES)
    vn3 = chunk_vnorm.reshape(depth, 1, d_c)
    bs_exp = jnp.repeat(jnp.swapaxes(chunk_bs, 1, 2), d_c // n_g, axis=2)
    w_out_b = w_out.astype(BF16)
    rw_p = jnp.pad(router_w, ((0, 0), (0, 0), (0, LANES - n_e)))
    rb_p = jnp.pad(router_b, ((0, 0), (0, LANES - n_e))).reshape(depth, 1, LANES)
    w_gu_b = w_gu.astype(BF16).reshape(depth * n_e, d, -1)
    w_dn_b = w_dn.astype(BF16).reshape(depth * n_e, -1, d)
    b_gu3 = b_gu.reshape(depth * n_e, 1, -1)
    b_dn3 = b_dn.reshape(depth * n_e, 1, d)
    cos_t, sin_t = _rope_tables(s_dec)
    ck4 = cache_k.reshape(b_dec, depth, p_len, d_diff)
    cv4 = cache_v.reshape(b_dec, depth, p_len, d_diff)
    s0_dec = jnp.transpose(state_wkv, (0, 1, 2, 5, 3, 4)).reshape(b_dec, depth, 2, HEAD_N, d_r)
    s0_ctx = jnp.zeros((b_ctx, 2, HEAD_N, d_r), F32)

    cond = jnp.concatenate([c_ctx[None, :], c], axis=0)
    m_rows = -(-cond.shape[0] // 8) * 8
    cond = jnp.pad(cond, ((0, m_rows - cond.shape[0]), (0, 0)))

    x = jnp.concatenate([x_prompt.reshape(t_ctx, d), x_sample.reshape(t_dec, d)], axis=0)
    bm = 512
    ks, vs, ss = [], [], []
    for l in range(depth):
        mods4 = ada_mods(cond, ada_w, ada_b3, l).reshape(m_rows, 6, 1, d)
        z = in_proj(x, mods4, norm_g3, w_in_p, l, t_ctx, s_dec)
        ks.append(z[:t_ctx, (col_q + 1) * d_diff:(col_q + 2) * d_diff])
        vs.append(z[:t_ctx, (col_q + 2) * d_diff:(col_q + 3) * d_diff])

        r, v, kk, w, kd, b, bonus, g = rwkv_prep(z, mu_p, pv, p2, w2bd, a2bd, g2p, bd3, l,
                                                 t_ctx, s_ctx, s_dec, d_r)
        y_c, sf_c = wkv_scan(r, kk, v, w, kd, b, s0_ctx, 0, b_ctx, s_ctx, d_r)
        y_d, _ = wkv_scan(r, kk, v, w, kd, b, s0_dec[:, l], t_ctx, b_dec, s_dec, d_r)
        ss.append(sf_c)
        o_r = rwkv_post(jnp.concatenate([y_c, y_d], axis=1), bonus, g, pv, bd3, l)

        lam_init = 0.8 - 0.6 * math.exp(-0.3 * l)
        lq1, lk1, lq2, lk2 = diff_lam[l].astype(F32)
        lam = (jnp.exp(jnp.sum(lq1 * lk1)) - jnp.exp(jnp.sum(lq2 * lk2)) + lam_init).reshape(1)
        od_c = diff_attn(z, lam, subln3, l, lam_init, 0, b_ctx, s_ctx, d_diff, col_q)
        od_d = diff_attn(z, lam, subln3, l, lam_init, t_ctx, b_dec, s_dec, d_diff, col_q,
                         ctx=(ck4, cv4, cos_t, sin_t))
        o_d = jnp.concatenate([od_c, od_d], axis=0)
        o_c = chunk_mix(z, vn3, chunk_ws, bs_exp, l, d_c, col_q + 3)

        x1, h2, logits = out_proj(x, o_r, o_d, o_c, w_out_b, norm_g3, mods4, rw_p, rb_p, l, t_ctx, s_dec)
        slot_tok, gates, block_e, block_valid, pos = _route(logits[:, :n_e], n_e, bm)
        xs = gather_rows(h2, slot_tok)
        ys = expert_ffn(xs, block_e, block_valid, w_gu_b, b_gu3, w_dn_b, b_dn3, l, n_e, bm)
        yk = gather_rows(ys, pos.reshape(-1, TOP_K).T.reshape(-1))
        x = moe_combine(x1, yk, gates, norm_g3, mods4, l, t_ctx, s_dec)

    n_hd = cache_k.shape[3]
    new_k = jnp.stack(ks, axis=1).reshape(b_ctx, s_ctx, depth, n_hd, 2, HEAD_N)
    new_k = jnp.transpose(new_k, (0, 2, 1, 3, 4, 5))
    new_v = jnp.stack(vs, axis=1).reshape(b_ctx, s_ctx, depth, n_hd, 2 * HEAD_N)
    new_v = jnp.transpose(new_v, (0, 2, 1, 3, 4))
    st = jnp.stack(ss, axis=1).reshape(b_ctx, depth, 2, HEAD_N, n_h, HEAD_N)
    new_s = jnp.transpose(st, (0, 1, 2, 4, 5, 3))
    y_prompt = x[:t_ctx].reshape(b_ctx, s_ctx, d)
    y_sample = x[t_ctx:].reshape(b_dec, s_dec, d)
    return (y_prompt, y_sample, new_k, new_v, new_s)
```

```python
import functools
import math

import jax
import jax.numpy as jnp
from jax import lax
from jax.experimental import pallas as pl
from jax.experimental.pallas import tpu as pltpu

F32 = jnp.float32
BF16 = jnp.bfloat16

LANES = 128
HEAD_N = 64
NORM_EPS = 1e-6
GN_EPS = 64e-5
ROPE_THETA = 10000.0
GRID_W = 64
TOP_K = 4
SWIGLU_LIMIT = 7.0
SWIGLU_ALPHA = 1.702
CHUNK = 128
VMEM_LIMIT = 56 * 1024 * 1024


def _cparams(sem):
    return pltpu.CompilerParams(dimension_semantics=sem, vmem_limit_bytes=VMEM_LIMIT)


def _split3(x):
    hi = x.astype(BF16)
    r1 = x - hi.astype(F32)
    mid = r1.astype(BF16)
    lo = (r1 - mid.astype(F32)).astype(BF16)
    return hi, mid, lo


def _segsum(x, bd3):
    outs = []
    for j in range(x.shape[1] // LANES):
        hi, mid, lo = _split3(x[:, j * LANES:(j + 1) * LANES])
        lhs = jnp.concatenate([hi, mid, lo], axis=1)
        outs.append(jnp.dot(lhs, bd3, preferred_element_type=F32))
    return outs[0] if len(outs) == 1 else jnp.concatenate(outs, axis=1)


def _rms(x, g):
    return x * lax.rsqrt(jnp.mean(x * x, axis=-1, keepdims=True) + NORM_EPS) * g


def _ada_kernel(c_ref, w_ref, b_ref, o_ref):
    c = c_ref[...]
    s = c * jax.nn.sigmoid(c)
    o_ref[...] = jnp.dot(s.astype(BF16), w_ref[...].astype(BF16),
                         preferred_element_type=F32) + b_ref[...]


def ada_mods(cond, ada_w, ada_b3, l):
    m, d = cond.shape
    n = ada_w.shape[2]
    tn = 1024
    return pl.pallas_call(
        _ada_kernel,
        grid=(n // tn,),
        in_specs=[pl.BlockSpec((m, d), lambda j: (0, 0)),
                  pl.BlockSpec((None, d, tn), lambda j: (l, 0, j)),
                  pl.BlockSpec((None, 1, tn), lambda j: (l, 0, j))],
        out_specs=pl.BlockSpec((m, tn), lambda j: (0, j)),
        out_shape=jax.ShapeDtypeStruct((m, n), F32),
        compiler_params=_cparams(("arbitrary",)),
        name="ada_mods",
    )(cond, ada_w, ada_b3)


def _inproj_kernel(x_ref, g_ref, sh_ref, sc_ref, w_ref, o_ref, h_ref):
    @pl.when(pl.program_id(1) == 0)
    def _():
        h = _rms(x_ref[...], g_ref[...]) * (1.0 + sc_ref[...]) + sh_ref[...]
        h_ref[...] = h.astype(BF16)

    o_ref[...] = jnp.dot(h_ref[...], w_ref[...], preferred_element_type=F32)


def _mod_row(i, tm, t_ctx, s_dec):
    r0 = i * tm
    return jnp.where(r0 < t_ctx, 0, 1 + (r0 - t_ctx) // s_dec)


def in_proj(x, mods4, norm_g3, w_in_p, l, t_ctx, s_dec):
    t, d = x.shape
    n = w_in_p.shape[2]
    tm, tn = math.gcd(1024, t_ctx, s_dec), 512
    mrow = lambda i: _mod_row(i, tm, t_ctx, s_dec)
    return pl.pallas_call(
        _inproj_kernel,
        grid=(t // tm, n // tn),
        in_specs=[pl.BlockSpec((tm, d), lambda i, j: (i, 0)),
                  pl.BlockSpec((None, 1, d), lambda i, j: (4 * l, 0, 0)),
                  pl.BlockSpec((None, None, 1, d), lambda i, j: (mrow(i), 0, 0, 0)),
                  pl.BlockSpec((None, None, 1, d), lambda i, j: (mrow(i), 1, 0, 0)),
                  pl.BlockSpec((None, d, tn), lambda i, j: (l, 0, j))],
        out_specs=pl.BlockSpec((tm, tn), lambda i, j: (i, j)),
        out_shape=jax.ShapeDtypeStruct((t, n), F32),
        scratch_shapes=[pltpu.VMEM((tm, d), BF16)],
        compiler_params=_cparams(("arbitrary", "arbitrary")),
        name="in_proj",
    )(x, norm_g3, mods4, mods4, w_in_p)


def _prep_kernel(z_ref, zp_ref, zn_ref, mu_ref, pv_ref, p2_ref, w2_ref, a2_ref, g2_ref, bd_ref,
                 r_ref, v_ref, kk_ref, w_ref, kd_ref, b_ref, bonus_ref, g_ref, zs_ref,
                 *, tm, t_ctx, s_ctx, s_dec, d_r):
    i = pl.program_id(0)
    r0 = i * tm
    in_ctx = r0 < t_ctx
    pos0 = jnp.where(in_ctx, r0 % s_ctx, (r0 - t_ctx) % s_dec)
    slen = jnp.where(in_ctx, s_ctx, s_dec)
    is_start = pos0 == 0
    is_end = pos0 + tm == slen
    width = z_ref.shape[1]
    rows = lax.broadcasted_iota(jnp.int32, (tm, LANES), 0)
    for j in range(width // LANES):
        sl = slice(j * LANES, (j + 1) * LANES)
        zc = z_ref[:, sl]
        prev_row = jnp.where(is_start, 0.0, zp_ref[7:8, sl])
        next_row = jnp.where(is_end, 0.0, zn_ref[0:1, sl])
        zprev = jnp.where(rows == 0, prev_row, pltpu.roll(zc, 1, axis=0))
        znext = jnp.where(rows == tm - 1, next_row, pltpu.roll(zc, tm - 1, axis=0))
        zs_ref[:, sl] = zc + mu_ref[0:1, sl] * (zprev - zc) + mu_ref[1:2, sl] * (znext - zc)

    bd3 = bd_ref[...]
    r = zs_ref[:, 0:d_r]
    k = zs_ref[:, d_r:2 * d_r]
    v = zs_ref[:, 2 * d_r:3 * d_r]
    o = 3 * d_r
    cw = zs_ref[:, o:o + LANES]
    ca = zs_ref[:, o + LANES:o + 2 * LANES]
    cg = zs_ref[:, o + 2 * LANES:o + 4 * LANES]
    k_k = pv_ref[0:1, :]
    k_a = pv_ref[1:2, :]
    r_k = pv_ref[2:3, :]

    r_ref[...] = r
    v_ref[...] = v
    g_ref[...] = jnp.dot(jax.nn.sigmoid(cg).astype(BF16), g2_ref[...], preferred_element_type=F32)

    kkr = k * k_k
    nrm = jnp.sqrt(_segsum(kkr * kkr, bd3))
    kk = kkr / jnp.maximum(nrm, 1e-12)
    kk_ref[...] = kk

    lw = p2_ref[0:1, :] + jnp.dot(jnp.tanh(cw).astype(BF16), w2_ref[...], preferred_element_type=F32)
    decay = -math.exp(-0.5) * jax.nn.sigmoid(lw)
    a = jax.nn.sigmoid(p2_ref[1:2, :] + jnp.dot(ca.astype(BF16), a2_ref[...], preferred_element_type=F32))
    kd_sum = None
    for d in range(2):
        a_d = a[:, d * d_r:(d + 1) * d_r]
        kd = k * (1.0 + (a_d - 1.0) * k_a)
        w_ref[d] = decay[:, d * d_r:(d + 1) * d_r]
        kd_ref[d] = kd
        b_ref[d] = kk * a_d
        kd_sum = kd if kd_sum is None else kd_sum + kd
    bonus_ref[...] = _segsum(r * kd_sum * r_k, bd3) * v


def rwkv_prep(z, mu_p, pv, p2, w2bd, a2bd, g2p, bd3, l, t_ctx, s_ctx, s_dec, d_r):
    t = z.shape[0]
    tm = 128
    wz = mu_p.shape[2]
    nb8 = t // 8
    kern = functools.partial(_prep_kernel, tm=tm, t_ctx=t_ctx, s_ctx=s_ctx, s_dec=s_dec, d_r=d_r)
    row = pl.BlockSpec((tm, d_r), lambda i: (i, 0))
    row2 = pl.BlockSpec((2, tm, d_r), lambda i: (0, i, 0))
    o1 = jax.ShapeDtypeStruct((t, d_r), F32)
    o2 = jax.ShapeDtypeStruct((2, t, d_r), F32)
    return pl.pallas_call(
        kern,
        grid=(t // tm,),
        in_specs=[pl.BlockSpec((tm, wz), lambda i: (i, 0)),
                  pl.BlockSpec((8, wz), lambda i: (jnp.maximum(i * (tm // 8) - 1, 0), 0)),
                  pl.BlockSpec((8, wz), lambda i: (jnp.minimum((i + 1) * (tm // 8), nb8 - 1), 0)),
                  pl.BlockSpec((None, 8, wz), lambda i: (l, 0, 0)),
                  pl.BlockSpec((None, 8, d_r), lambda i: (l, 0, 0)),
                  pl.BlockSpec((None, 8, 2 * d_r), lambda i: (l, 0, 0)),
                  pl.BlockSpec((None, LANES, 2 * d_r), lambda i: (l, 0, 0)),
                  pl.BlockSpec((None, LANES, 2 * d_r), lambda i: (l, 0, 0)),
                  pl.BlockSpec((None, 2 * LANES, d_r), lambda i: (l, 0, 0)),
                  pl.BlockSpec((3 * LANES, LANES), lambda i: (0, 0))],
        out_specs=[row, row, row, row2, row2, row2, row, row],
        out_shape=[o1, o1, o1, o2, o2, o2, o1, o1],
        scratch_shapes=[pltpu.VMEM((tm, wz), F32)],
        compiler_params=_cparams(("arbitrary",)),
        name="rwkv_prep",
    )(z, z, z, mu_p, pv, p2, w2bd, a2bd, g2p, bd3)


def _split2(x):
    hi = x.astype(BF16)
    lo = (x - hi.astype(F32)).astype(BF16)
    return hi, lo


def _lhs3(a, axis):
    ah, al = _split2(a)
    return jnp.concatenate([ah, ah, al], axis=axis)


def _rhs3(b, axis):
    bh, bl = _split2(b)
    return jnp.concatenate([bh, bl, bh], axis=axis)


def _dot3(l3, r3):
    return jnp.dot(l3, r3, preferred_element_type=F32)


def _dot3_nt(l3, r3):
    return lax.dot_general(l3, r3, (((1,), (1,)), ((), ())), preferred_element_type=F32)


def _dot3_tn(l3, r3):
    return lax.dot_general(l3, r3, (((0,), (0,)), ((), ())), preferred_element_type=F32)


def _mm1(a, b):
    return jnp.dot(a.astype(BF16), b.astype(BF16), preferred_element_type=F32)


SCAN_C = 64


def _scan_kernel(r_ref, kk_ref, v_ref, lw_ref, kd_ref, b_ref, s0_ref,
                 y_ref, sf_ref, s_ref, kr_s, t_s, g4_s, gv_s, kb_s, gc_s, *, tb, nt, nlt):
    c_len = SCAN_C
    fwd = pl.program_id(1) == 0
    tstep = pl.program_id(3)
    lane = lax.broadcasted_iota(jnp.int32, (c_len, LANES), 1)
    row = lax.broadcasted_iota(jnp.int32, (c_len, LANES), 0)
    colm = lane & (HEAD_N - 1)
    ahead = (row - colm) * jnp.where(fwd, 1, -1)
    strict = ahead > 0
    incl = ahead >= 0
    h0 = lane < HEAD_N
    lane2 = lax.broadcasted_iota(jnp.int32, (2 * c_len, LANES), 1)
    row2 = lax.broadcasted_iota(jnp.int32, (2 * c_len, LANES), 0)
    bdmask = (lane2 < HEAD_N) == (row2 < HEAD_N)
    eye = lane2 == row2
    m_incl = jnp.where(jnp.logical_and(incl, h0), 1.0, 0.0).astype(BF16)
    m_incl3 = jnp.concatenate([m_incl, m_incl, m_incl], axis=1)
    zpad = jnp.zeros((c_len, LANES), BF16)

    def two_heads(x):
        return jnp.concatenate([jnp.where(h0, x, 0.0), jnp.where(h0, 0.0, x)], axis=0)

    @pl.when(tstep == 0)
    def _():
        for j in range(nlt):
            s_ref[j] = two_heads(s0_ref[:, j * LANES:(j + 1) * LANES])

    n_ch = tb // c_len

    def chunk_rows(ci):
        cc = jnp.where(fwd, ci, n_ch - 1 - ci)
        return pl.ds(pl.multiple_of(cc * c_len, c_len), c_len)

    tiles = range(nlt)
    lanes_of = [slice(j * LANES, (j + 1) * LANES) for j in tiles]
    eye_f = jnp.where(eye, 1.0, 0.0)
    for ci in range(n_ch):
        rows = chunk_rows(ci)
        lws = [lw_ref[rows, sl] for sl in lanes_of]
        lgs = []
        for lw in lws:
            hi, mid, lo = _split3(lw)
            lgs.append(jnp.dot(m_incl3, jnp.concatenate([hi, zpad, mid, zpad, lo, zpad], axis=0),
                               preferred_element_type=F32))
        lasts = [jnp.where(fwd, lg[c_len - 1:c_len, :], lg[0:1, :]) for lg in lgs]
        kr3s = [_lhs3(jnp.concatenate([kk_ref[rows, sl] * jnp.exp(lg - lw), r_ref[rows, sl] * jnp.exp(lg)],
                                      axis=0), 1)
                for sl, lg, lw in zip(lanes_of, lgs, lws)]
        ens = [jnp.exp(-lg) for lg in lgs]
        gabs = [_dot3_nt(kr3, jnp.concatenate([_rhs3(two_heads(kd_ref[rows, sl] * en), 1),
                                               _rhs3(two_heads(b_ref[rows, sl] * en), 1)], axis=0))
                for kr3, sl, en in zip(kr3s, lanes_of, ens)]
        gas = [gab[:, :LANES] for gab in gabs]
        gbs = [gab[:, LANES:] for gab in gabs]
        ps = [-two_heads(jnp.where(strict, gb[:c_len], 0.0)) for gb in gbs]
        accs = [eye_f + p for p in ps]
        ps = [_mm1(p, p) for p in ps]
        span = 4
        while span < c_len:
            both = [_mm1(jnp.concatenate([p, acc], axis=0), p) for p, acc in zip(ps, accs)]
            accs = [acc + b2[2 * c_len:] for acc, b2 in zip(accs, both)]
            ps = [b2[:2 * c_len] for b2 in both]
            span *= 2
        accs = [acc + _mm1(acc, p) for acc, p in zip(accs, ps)]
        gvs = [_dot3(_lhs3(jnp.concatenate([jnp.where(strict, ga[:c_len], 0.0),
                                            jnp.where(incl, ga[c_len:], 0.0)], axis=0), 1),
                     _rhs3(two_heads(v_ref[rows, sl]), 0))
               for ga, sl in zip(gas, lanes_of)]
        for j in tiles:
            q = ci * nlt + j
            ec = jnp.exp(lasts[j] - lgs[j])
            kr_s[q] = kr3s[j]
            t_s[q] = _lhs3(accs[j][:c_len] + accs[j][c_len:], 1)
            g4_s[q] = _lhs3(jnp.where(incl, gbs[j][c_len:], 0.0), 1)
            gv_s[q] = gvs[j]
            kb_s[q] = _lhs3(jnp.concatenate([kd_ref[rows, lanes_of[j]] * ec, b_ref[rows, lanes_of[j]] * ec],
                                            axis=0), 0)
            gcol = jnp.transpose(jnp.broadcast_to(jnp.exp(lasts[j]), (2 * c_len, LANES)))
            gc_s[q] = jnp.where(bdmask, gcol, 0.0)

    for ci in range(n_ch):
        rows = chunk_rows(ci)
        qs = [ci * nlt + j for j in tiles]
        bdss = [s_ref[j] for j in tiles]
        xrs = [_dot3(kr_s[q], _rhs3(bds, 0)) for q, bds in zip(qs, bdss)]
        us = [_dot3(t_s[q], _rhs3(two_heads(xr[:c_len] + gv_s[q, :c_len]), 0)) for q, xr in zip(qs, xrs)]
        news = [_dot3_tn(kb_s[q], _rhs3(jnp.concatenate([v_ref[rows, sl], -u], axis=0), 0))
                for q, sl, u in zip(qs, lanes_of, us)]
        for j in tiles:
            s_ref[j] = jnp.where(bdmask, news[j], 0.0) + bdss[j] * gc_s[qs[j]]
        for j in tiles:
            y_ref[rows, lanes_of[j]] = (xrs[j][c_len:] + gv_s[qs[j], c_len:]
                                        - _dot3(g4_s[qs[j]], _rhs3(two_heads(us[j]), 0)))

    @pl.when(tstep == nt - 1)
    def _():
        for j in range(nlt):
            bds = s_ref[j]
            sf_ref[:, j * LANES:(j + 1) * LANES] = bds[:HEAD_N] + bds[HEAD_N:]


def wkv_scan(r, kk, v, lw, kd, b, s0, row0, n_seq, s_len, d_r):
    tb = min(256, s_len)
    nt = s_len // tb
    blk0 = row0 // tb
    nlt = 8
    wl = nlt * LANES
    nq = (tb // SCAN_C) * nlt

    def tblk(bi, d, c):
        return blk0 + bi * nt + c + d * (nt - 1 - 2 * c)

    shared = pl.BlockSpec((tb, wl), lambda bi, d, hp, c: (tblk(bi, d, c), hp))
    perdir = pl.BlockSpec((None, tb, wl), lambda bi, d, hp, c: (d, tblk(bi, d, c), hp))
    state = pl.BlockSpec((None, None, HEAD_N, wl), lambda bi, d, hp, c: (bi, d, 0, hp))
    kern = functools.partial(_scan_kernel, tb=tb, nt=nt, nlt=nlt)
    return pl.pallas_call(
        kern,
        grid=(n_seq, 2, d_r // wl, nt),
        in_specs=[shared, shared, shared, perdir, perdir, perdir, state],
        out_specs=[pl.BlockSpec((None, tb, wl),
                                lambda bi, d, hp, c: (d, bi * nt + c + d * (nt - 1 - 2 * c), hp)),
                   state],
        out_shape=[jax.ShapeDtypeStruct((2, n_seq * s_len, d_r), F32),
                   jax.ShapeDtypeStruct((n_seq, 2, HEAD_N, d_r), F32)],
        scratch_shapes=[pltpu.VMEM((nlt, 2 * HEAD_N, LANES), F32),
                        pltpu.VMEM((nq, 2 * SCAN_C, 3 * LANES), BF16),
                        pltpu.VMEM((nq, SCAN_C, 3 * LANES), BF16),
                        pltpu.VMEM((nq, SCAN_C, 3 * LANES), BF16),
                        pltpu.VMEM((nq, 2 * SCAN_C, LANES), F32),
                        pltpu.VMEM((nq, 6 * SCAN_C, LANES), BF16),
                        pltpu.VMEM((nq, 2 * HEAD_N, LANES), F32)],
        compiler_params=_cparams(("arbitrary", "arbitrary", "arbitrary", "arbitrary")),
        name="wkv_scan",
    )(r, kk, v, lw, kd, b, s0)


def _post_kernel(y_ref, bonus_ref, g_ref, pv_ref, bd_ref, o_ref):
    bd3 = bd_ref[...]
    y = y_ref[0] + y_ref[1]
    mu = _segsum(y, bd3) * (1.0 / HEAD_N)
    yc = y - mu
    var = _segsum(yc * yc, bd3) * (1.0 / HEAD_N)
    yn = yc * lax.rsqrt(var + GN_EPS) * pv_ref[3:4, :] + pv_ref[4:5, :]
    o_ref[...] = (yn + bonus_ref[...]) * g_ref[...]


def rwkv_post(y, bonus, g, pv, bd3, l):
    t, d_r = bonus.shape
    tm = 256
    row = pl.BlockSpec((tm, d_r), lambda i: (i, 0))
    return pl.pallas_call(
        _post_kernel,
        grid=(t // tm,),
        in_specs=[pl.BlockSpec((2, tm, d_r), lambda i: (0, i, 0)), row, row,
                  pl.BlockSpec((None, 8, d_r), lambda i: (l, 0, 0)),
                  pl.BlockSpec((3 * LANES, LANES), lambda i: (0, 0))],
        out_specs=row,
        out_shape=jax.ShapeDtypeStruct((t, d_r), F32),
        compiler_params=_cparams(("arbitrary",)),
        name="rwkv_post",
    )(y, bonus, g, pv, bd3)


def _rope(x, cos, sin):
    lane = lax.broadcasted_iota(jnp.int32, (x.shape[0], LANES), 1)
    first = (lane & 31) < 16
    outs = []
    for j in range(x.shape[1] // LANES):
        xc = x[:, j * LANES:(j + 1) * LANES]
        partner = jnp.where(first, pltpu.roll(xc, LANES - 16, axis=1), pltpu.roll(xc, 16, axis=1))
        outs.append(xc * cos + partner * sin)
    return outs[0] if len(outs) == 1 else jnp.concatenate(outs, axis=1)


def _attn_kernel(lam_ref, q_ref, k_ref, v_ref, *rest, use_ctx, n_heads, out_scale):
    if use_ctx:
        (ck_ref, cv_ref, cosq_ref, sinq_ref, cosk_ref, sink_ref, g_ref,
         o_ref, kb_ref, vb_ref, ckb_ref, cvb_ref) = rest
    else:
        g_ref, o_ref, kb_ref, vb_ref = rest
    qi = pl.program_id(1)

    @pl.when(qi == 0)
    def _():
        k = k_ref[...]
        if use_ctx:
            k = _rope(k, cosk_ref[...], sink_ref[...])
            ckb_ref[...] = ck_ref[...].astype(BF16)
            cvb_ref[...] = cv_ref[...].astype(BF16)
        kb_ref[...] = k.astype(BF16)
        vb_ref[...] = v_ref[...].astype(BF16)

    lam = lam_ref[0]
    q = q_ref[...]
    if use_ctx:
        q = _rope(q, cosq_ref[...], sinq_ref[...])
    tq = q.shape[0]
    lane = lax.broadcasted_iota(jnp.int32, (tq, LANES), 1)
    scale = HEAD_N ** -0.5
    dn = (((1,), (1,)), ((), ()))
    for h in range(n_heads):
        sl = slice(h * LANES, (h + 1) * LANES)
        qh = q[:, sl] * scale
        kh = kb_ref[:, sl]
        outs = []
        for comp in range(2):
            qc = jnp.where((lane < HEAD_N) == (comp == 0), qh, 0.0).astype(BF16)
            s_a = lax.dot_general(qc, kh, dn, preferred_element_type=F32)
            m = jnp.max(s_a, axis=-1, keepdims=True)
            if use_ctx:
                s_b = lax.dot_general(qc, ckb_ref[:, sl], dn, preferred_element_type=F32)
                m = jnp.maximum(m, jnp.max(s_b, axis=-1, keepdims=True))
            e_a = jnp.exp(s_a - m)
            den = jnp.sum(e_a, axis=-1, keepdims=True)
            acc = jnp.dot(e_a.astype(BF16), vb_ref[:, sl], preferred_element_type=F32)
            if use_ctx:
                e_b = jnp.exp(s_b - m)
                den = den + jnp.sum(e_b, axis=-1, keepdims=True)
                acc = acc + jnp.dot(e_b.astype(BF16), cvb_ref[:, sl], preferred_element_type=F32)
            outs.append(acc / den)
        o = outs[0] - lam * outs[1]
        o_ref[:, sl] = _rms(o, g_ref[...]) * out_scale


def diff_attn(z, lam, subln3, l, lam_init, row0, n_seq, s_len, d_diff, col0, ctx=None):
    tq = 256
    nq = s_len // tq
    n_heads = d_diff // LANES
    qb0 = row0 // tq
    sb0 = row0 // s_len
    use_ctx = ctx is not None
    in_specs = [pl.BlockSpec(memory_space=pltpu.SMEM),
                pl.BlockSpec((tq, d_diff), lambda bi, qi: (qb0 + bi * nq + qi, col0)),
                pl.BlockSpec((s_len, d_diff), lambda bi, qi: (sb0 + bi, col0 + 1)),
                pl.BlockSpec((s_len, d_diff), lambda bi, qi: (sb0 + bi, col0 + 2))]
    args = [lam, z, z, z]
    scratch = [pltpu.VMEM((s_len, d_diff), BF16), pltpu.VMEM((s_len, d_diff), BF16)]
    if use_ctx:
        cache_k, cache_v, cos, sin = ctx
        p_len = cache_k.shape[2]
        cspec = pl.BlockSpec((None, None, p_len, d_diff), lambda bi, qi: (bi, l, 0, 0))
        in_specs += [cspec, cspec,
                     pl.BlockSpec((tq, LANES), lambda bi, qi: (qi, 0)),
                     pl.BlockSpec((tq, LANES), lambda bi, qi: (qi, 0)),
                     pl.BlockSpec((s_len, LANES), lambda bi, qi: (0, 0)),
                     pl.BlockSpec((s_len, LANES), lambda bi, qi: (0, 0))]
        args += [cache_k, cache_v, cos, sin, cos, sin]
        scratch += [pltpu.VMEM((p_len, d_diff), BF16), pltpu.VMEM((p_len, d_diff), BF16)]
    in_specs.append(pl.BlockSpec((None, 1, LANES), lambda bi, qi: (l, 0, 0)))
    args.append(subln3)
    kern = functools.partial(_attn_kernel, use_ctx=use_ctx, n_heads=n_heads,
                             out_scale=1.0 - lam_init)
    return pl.pallas_call(
        kern,
        grid=(n_seq, nq),
        in_specs=in_specs,
        out_specs=pl.BlockSpec((tq, d_diff), lambda bi, qi: (bi * nq + qi, 0)),
        out_shape=jax.ShapeDtypeStruct((n_seq * s_len, d_diff), F32),
        scratch_shapes=scratch,
        compiler_params=_cparams(("arbitrary", "arbitrary")),
        name="diff_attn_ctx" if use_ctx else "diff_attn",
    )(*args)


def _chunk_kernel(u_ref, g_ref, vn_ref, ws_ref, bs_ref, o_ref):
    u = jax.nn.gelu(u_ref[...], approximate=True)
    vv = _rms(jax.nn.gelu(g_ref[...], approximate=True), vn_ref[...])
    for gi in range(ws_ref.shape[0]):
        sl = slice(gi * LANES, (gi + 1) * LANES)
        s = jnp.dot(ws_ref[gi].astype(BF16), vv[:, sl].astype(BF16), preferred_element_type=F32)
        o_ref[:, sl] = u[:, sl] * (s + bs_ref[:, sl])


def chunk_mix(z, vn3, chunk_ws, bs_exp, l, d_c, col_u):
    t = z.shape[0]
    n_g = chunk_ws.shape[1]
    return pl.pallas_call(
        _chunk_kernel,
        grid=(t // CHUNK,),
        in_specs=[pl.BlockSpec((CHUNK, d_c), lambda i: (i, col_u)),
                  pl.BlockSpec((CHUNK, d_c), lambda i: (i, col_u + 1)),
                  pl.BlockSpec((None, 1, d_c), lambda i: (l, 0, 0)),
                  pl.BlockSpec((None, n_g, CHUNK, CHUNK), lambda i: (l, 0, 0, 0)),
                  pl.BlockSpec((None, CHUNK, d_c), lambda i: (l, 0, 0))],
        out_specs=pl.BlockSpec((CHUNK, d_c), lambda i: (i, 0)),
        out_shape=jax.ShapeDtypeStruct((t, d_c), F32),
        compiler_params=_cparams(("arbitrary",)),
        name="chunk_mix",
    )(z, z, vn3, chunk_ws, bs_exp)


def _outproj_kernel(x_ref, or_ref, od_ref, oc_ref, wr_ref, wd_ref, wc_ref, g1n_ref, g2n_ref,
                    gate_ref, sh_ref, sc_ref, rw_ref, rb_ref, x1_ref, h2_ref, lg_ref):
    mix = jnp.dot(or_ref[...].astype(BF16), wr_ref[...], preferred_element_type=F32)
    mix = mix + jnp.dot(od_ref[...].astype(BF16), wd_ref[...], preferred_element_type=F32)
    mix = mix + jnp.dot(oc_ref[...].astype(BF16), wc_ref[...], preferred_element_type=F32)
    x1 = x_ref[...] + gate_ref[...] * _rms(mix, g1n_ref[...])
    x1_ref[...] = x1
    h2 = _rms(x1, g2n_ref[...]) * (1.0 + sc_ref[...]) + sh_ref[...]
    half = h2.shape[1] // 2
    bits = lax.bitcast_convert_type(h2.astype(BF16).astype(F32), jnp.uint32)
    h2_ref[...] = (bits[:, :half] >> 16) | bits[:, half:]
    lg_ref[...] = _dot3(_lhs3(h2, 1), _rhs3(rw_ref[...], 0)) + rb_ref[...]


def out_proj(x, o_r, o_d, o_c, w_out_b, norm_g3, mods4, rw_p, rb_p, l, t_ctx, s_dec):
    t, d = x.shape
    d_r, d_d, d_c = o_r.shape[1], o_d.shape[1], o_c.shape[1]
    tm = math.gcd(256, t_ctx, s_dec)
    mrow = lambda i: _mod_row(i, tm, t_ctx, s_dec)
    nrm = lambda k: pl.BlockSpec((None, 1, d), lambda i: (4 * l + k, 0, 0))
    mod = lambda k: pl.BlockSpec((None, None, 1, d), lambda i: (mrow(i), k, 0, 0))
    row = lambda w: pl.BlockSpec((tm, w), lambda i: (i, 0))
    nrb = d_r // d_d
    return pl.pallas_call(
        _outproj_kernel,
        grid=(t // tm,),
        in_specs=[row(d), row(d_r), row(d_d), row(d_c),
                  pl.BlockSpec((None, d_r, d), lambda i: (l, 0, 0)),
                  pl.BlockSpec((None, d_d, d), lambda i: (l, nrb, 0)),
                  pl.BlockSpec((None, d_c, d), lambda i: (l, nrb + 1, 0)),
                  nrm(1), nrm(2), mod(2), mod(3), mod(4),
                  pl.BlockSpec((None, d, LANES), lambda i: (l, 0, 0)),
                  pl.BlockSpec((None, 1, LANES), lambda i: (l, 0, 0))],
        out_specs=[row(d), row(d // 2), row(LANES)],
        out_shape=[jax.ShapeDtypeStruct((t, d), F32), jax.ShapeDtypeStruct((t, d // 2), jnp.uint32),
                   jax.ShapeDtypeStruct((t, LANES), F32)],
        compiler_params=_cparams(("arbitrary",)),
        name="out_proj",
    )(x, o_r, o_d, o_c, w_out_b, w_out_b, w_out_b, norm_g3, norm_g3, mods4, mods4, mods4, rw_p, rb_p)


def _gather_kernel(idx_ref, src_ref, o_ref, sem, *, bm):
    def row_copy(r, src_row):
        return pltpu.make_async_copy(src_ref.at[pl.ds(src_row, 1), :], o_ref.at[pl.ds(r, 1), :], sem)

    def issue(i, carry):
        r = 2 * i
        row_copy(r, idx_ref[0, r]).start(priority=0)
        row_copy(r + 1, idx_ref[0, r + 1]).start(priority=1)
        return carry

    lax.fori_loop(0, bm // 2, issue, 0)
    pltpu.make_async_copy(o_ref, o_ref, sem).wait()


def gather_rows(src, idx, bm=512):
    n, d = src.shape
    m = idx.shape[0]
    return pl.pallas_call(
        functools.partial(_gather_kernel, bm=bm),
        grid=(m // bm,),
        in_specs=[pl.BlockSpec((None, 1, bm), lambda i: (i, 0, 0), memory_space=pltpu.SMEM),
                  pl.BlockSpec(memory_space=pl.ANY)],
        out_specs=pl.BlockSpec((bm, d), lambda i: (i, 0)),
        out_shape=jax.ShapeDtypeStruct((m, d), src.dtype),
        scratch_shapes=[pltpu.SemaphoreType.DMA],
        compiler_params=_cparams(("arbitrary",)),
        name="gather_rows",
    )(idx.reshape(m // bm, 1, bm), src)


def _expert_kernel(be_ref, bv_ref, idx_ref, nidx_ref, h2_ref, wg_ref, wu_ref, bg_ref, bu_ref, wd_ref,
                   bdn_ref, o_ref, xb_ref, xg_ref, sem, *, nf, nb, bm):
    i = pl.program_id(0)
    f = pl.program_id(1)
    slot = i % 2
    per_step = bm // nf

    def row_copy(src_idx_ref, row, s):
        return pltpu.make_async_copy(h2_ref.at[pl.ds(src_idx_ref[0, row], 1), :],
                                     xg_ref.at[s, pl.ds(row, 1), :], sem.at[s])

    def block_wait(s):
        pltpu.make_async_copy(xg_ref.at[s], xg_ref.at[s], sem.at[s]).wait()

    @pl.when(jnp.logical_and(i == 0, f == 0))
    def _():
        def issue(j, carry):
            row_copy(idx_ref, 2 * j, 0).start(priority=0)
            row_copy(idx_ref, 2 * j + 1, 0).start(priority=1)
            return carry
        lax.fori_loop(0, bm // 2, issue, 0)

    @pl.when(f == 0)
    def _():
        block_wait(slot)
        p = xg_ref[slot]
        half = p.shape[1]
        xb_ref[:, :half] = lax.bitcast_convert_type(p << 16, F32).astype(BF16)
        xb_ref[:, half:] = lax.bitcast_convert_type(p & jnp.uint32(0xFFFF0000), F32).astype(BF16)
        o_ref[...] = jnp.zeros_like(o_ref)

    row0 = f * per_step

    @pl.when(bv_ref[i] > 0)
    def _():
        for r in range(per_step):
            row_copy(nidx_ref, row0 + r, 1 - slot).start(priority=r % 2)
        xb = xb_ref[...]
        gt = jnp.dot(xb, wg_ref[...], preferred_element_type=F32) + bg_ref[...]
        up = jnp.dot(xb, wu_ref[...], preferred_element_type=F32) + bu_ref[...]
        gt = jnp.minimum(gt, SWIGLU_LIMIT)
        up = jnp.clip(up, -SWIGLU_LIMIT, SWIGLU_LIMIT)
        act = (up + 1.0) * gt * jax.nn.sigmoid(SWIGLU_ALPHA * gt)
        o_ref[...] += jnp.dot(act.astype(BF16), wd_ref[...], preferred_element_type=F32)

    @pl.when(bv_ref[i] <= 0)
    def _():
        def issue(j, carry):
            row_copy(nidx_ref, row0 + 2 * j, 1 - slot).start(priority=0)
            row_copy(nidx_ref, row0 + 2 * j + 1, 1 - slot).start(priority=1)
            return carry
        lax.fori_loop(0, per_step // 2, issue, 0)

    @pl.when(f == nf - 1)
    def _():
        o_ref[...] = o_ref[...] + bdn_ref[...]

    @pl.when(jnp.logical_and(i == nb - 1, f == nf - 1))
    def _():
        block_wait(1 - slot)


def expert_ffn(h2p, slot_tok, block_e, block_valid, w_gu_b, b_gu3, w_dn_b, b_dn3, l, n_e, bm):
    d_half = h2p.shape[1]
    n_slots = slot_tok.shape[0]
    d = 2 * d_half
    ff = w_dn_b.shape[1]
    ft = 512
    nf = ff // ft
    nb = n_slots // bm
    e0 = l * n_e
    idx3 = slot_tok.reshape(nb, 1, bm)

    def fsel(i, f, bv):
        return jnp.where(bv[i] > 0, f, nf - 1)

    return pl.pallas_call(
        functools.partial(_expert_kernel, nf=nf, nb=nb, bm=bm),
        grid_spec=pltpu.PrefetchScalarGridSpec(
            num_scalar_prefetch=2,
            grid=(nb, nf),
            in_specs=[pl.BlockSpec((None, 1, bm), lambda i, f, be, bv: (i, 0, 0), memory_space=pltpu.SMEM),
                      pl.BlockSpec((None, 1, bm), lambda i, f, be, bv: (jnp.minimum(i + 1, nb - 1), 0, 0),
                                   memory_space=pltpu.SMEM),
                      pl.BlockSpec(memory_space=pl.ANY),
                      pl.BlockSpec((None, d, ft), lambda i, f, be, bv: (e0 + be[i], 0, fsel(i, f, bv))),
                      pl.BlockSpec((None, d, ft), lambda i, f, be, bv: (e0 + be[i], 0, nf + fsel(i, f, bv))),
                      pl.BlockSpec((None, 1, ft), lambda i, f, be, bv: (e0 + be[i], 0, fsel(i, f, bv))),
                      pl.BlockSpec((None, 1, ft), lambda i, f, be, bv: (e0 + be[i], 0, nf + fsel(i, f, bv))),
                      pl.BlockSpec((None, ft, d), lambda i, f, be, bv: (e0 + be[i], fsel(i, f, bv), 0)),
                      pl.BlockSpec((None, 1, d), lambda i, f, be, bv: (e0 + be[i], 0, 0))],
            out_specs=pl.BlockSpec((bm, d), lambda i, f, be, bv: (i, 0)),
            scratch_shapes=[pltpu.VMEM((bm, d), BF16),
                            pltpu.VMEM((2, bm, d_half), jnp.uint32),
                            pltpu.SemaphoreType.DMA((2,))]),
        out_shape=jax.ShapeDtypeStruct((n_slots, d), F32),
        compiler_params=_cparams(("arbitrary", "arbitrary")),
        name="expert_ffn",
    )(block_e, block_valid, idx3, idx3, h2p, w_gu_b, w_gu_b, b_gu3, b_gu3, w_dn_b, b_dn3)


def _combine_kernel(x_ref, *rest):
    y_refs = rest[:TOP_K]
    rg_ref, gn_ref, gate_ref, o_ref = rest[TOP_K:]
    f = y_refs[0][...] * rg_ref[:, 0:1]
    for k in range(1, TOP_K):
        f = f + y_refs[k][...] * rg_ref[:, k:k + 1]
    o_ref[...] = x_ref[...] + gate_ref[...] * _rms(f, gn_ref[...])


def moe_combine(x1, yk, gates, norm_g3, mods4, l, t_ctx, s_dec):
    t, d = x1.shape
    tm = 256
    nblk = t // tm
    mrow = lambda i: _mod_row(i, tm, t_ctx, s_dec)
    y_specs = [pl.BlockSpec((tm, d), functools.partial(lambda i, k: (k * nblk + i, 0), k=k))
               for k in range(TOP_K)]
    return pl.pallas_call(
        _combine_kernel,
        grid=(nblk,),
        in_specs=[pl.BlockSpec((tm, d), lambda i: (i, 0)),
                  *y_specs,
                  pl.BlockSpec((tm, TOP_K), lambda i: (i, 0)),
                  pl.BlockSpec((None, 1, d), lambda i: (4 * l + 3, 0, 0)),
                  pl.BlockSpec((None, None, 1, d), lambda i: (mrow(i), 5, 0, 0))],
        out_specs=pl.BlockSpec((tm, d), lambda i: (i, 0)),
        out_shape=jax.ShapeDtypeStruct((t, d), F32),
        compiler_params=_cparams(("arbitrary",)),
        name="moe_combine",
    )(x1, *([yk] * TOP_K), gates, norm_g3, mods4)


def _route(logits, n_e, bm):
    t = logits.shape[0]
    top_v, top_i = lax.top_k(logits, TOP_K)
    gates = jax.nn.softmax(top_v, axis=-1)
    tk = t * TOP_K
    flat_e = top_i.reshape(-1).astype(jnp.int32)
    eids = jnp.arange(n_e, dtype=jnp.int32)
    iota = jnp.arange(tk, dtype=jnp.int32)
    sorted_e, order = lax.sort_key_val(flat_e, iota)
    counts = jnp.sum((flat_e[:, None] == eids[None, :]).astype(jnp.int32), axis=0)
    padded = ((counts + bm - 1) // bm) * bm
    pad_end = jnp.cumsum(padded)
    pad_start = pad_end - padded
    start = jnp.cumsum(counts) - counts
    shift = pad_start - start
    dest = iota + jnp.sum(jnp.where(sorted_e[:, None] == eids[None, :], shift[None, :], 0), axis=1)
    _, pos = lax.sort_key_val(order, dest)
    nb = (tk + n_e * (bm - 1) + bm - 1) // bm
    starts = jnp.arange(nb, dtype=jnp.int32) * bm
    block_e = jnp.minimum(jnp.sum((pad_end[None, :] <= starts[:, None]).astype(jnp.int32), axis=1), n_e - 1)
    block_valid = (starts < pad_end[-1]).astype(jnp.int32)
    src = starts[:, None] + jnp.arange(bm, dtype=jnp.int32)[None, :] - shift[block_e][:, None]
    live = jnp.logical_and(src < (start + counts)[block_e][:, None], block_valid[:, None] > 0)
    slot_tok = jnp.where(live, order[jnp.clip(src, 0, tk - 1)] // TOP_K, 0).reshape(-1)
    last_e = jnp.max(jnp.where(counts > 0, eids, 0))
    block_e = jnp.where(block_valid > 0, block_e, last_e)
    return slot_tok, gates, block_e, block_valid, pos


def _rope_tables(s_len):
    rows = s_len // GRID_W
    row = jnp.repeat(jnp.arange(rows), GRID_W)
    col = jnp.tile(jnp.arange(GRID_W), rows)
    half = HEAD_N // 2
    inv_freq = 1.0 / (ROPE_THETA ** (jnp.arange(0, half, 2, dtype=F32) / half))
    ang = jnp.stack([row, col], axis=-1).astype(F32)[..., None] * inv_freq
    cos, sin = jnp.cos(ang), jnp.sin(ang)
    cos64 = jnp.concatenate([cos[:, 0], cos[:, 0], cos[:, 1], cos[:, 1]], axis=-1)
    sin64 = jnp.concatenate([-sin[:, 0], sin[:, 0], -sin[:, 1], sin[:, 1]], axis=-1)
    return jnp.tile(cos64, (1, 2)), jnp.tile(sin64, (1, 2))


def kernel(x_prompt, x_sample, c, cache_k, cache_v, state_wkv, c_ctx, ada_w, ada_b, norm_g, w_in, shift_mu, decay_w0, decay_w2, iclr_a0, iclr_a2, gate_g2, k_k, k_a, r_k, ln_x_w, ln_x_b, diff_lam, diff_subln, chunk_vnorm, chunk_ws, chunk_bs, w_out, router_w, router_b, w_gu, b_gu, w_dn, b_dn):
    b_ctx, s_ctx, d = x_prompt.shape
    b_dec, s_dec, _ = x_sample.shape
    depth = ada_w.shape[0]
    t_ctx, t_dec = b_ctx * s_ctx, b_dec * s_dec
    d_r = k_k.shape[1]
    n_h = d_r // HEAD_N
    d_rin = shift_mu.shape[2]
    d_diff = diff_subln.shape[1] * (cache_k.shape[3])
    n_g = chunk_ws.shape[1]
    d_c = n_g * chunk_ws.shape[2]
    n_e = router_w.shape[2]
    d_lw, d_la, d_lg = decay_w2.shape[2], iclr_a2.shape[2], gate_g2.shape[1]
    p_len = cache_k.shape[2]
    assert d_lw == HEAD_N and d_la == HEAD_N and d_lg <= 2 * LANES and d_diff == d_c
    wz = 3 * d_r + 4 * LANES
    col_q = wz // d_diff
    assert wz % d_diff == 0 and d_rin <= wz

    padc = wz - d_rin
    w_in_p = jnp.concatenate([w_in[:, :, :d_rin], jnp.zeros((depth, d, padc), F32), w_in[:, :, d_rin:]],
                             axis=2).astype(BF16)
    mu_p = jnp.pad(shift_mu, ((0, 0), (0, 6), (0, padc)))
    pv = jnp.stack([k_k, k_a, r_k.reshape(depth, d_r), ln_x_w, ln_x_b], axis=1)
    pv = jnp.pad(pv, ((0, 0), (0, 3), (0, 0)))
    p2 = jnp.stack([decay_w0.reshape(depth, 2 * d_r), iclr_a0.reshape(depth, 2 * d_r)], axis=1)
    p2 = jnp.pad(p2, ((0, 0), (0, 6), (0, 0)))
    zl = jnp.zeros((depth, HEAD_N, d_r), F32)
    w2bd = jnp.concatenate([jnp.concatenate([decay_w2[:, 0], zl], axis=2),
                            jnp.concatenate([zl, decay_w2[:, 1]], axis=2)], axis=1).astype(BF16)
    a2bd = jnp.concatenate([jnp.concatenate([iclr_a2[:, 0], zl], axis=2),
                            jnp.concatenate([zl, iclr_a2[:, 1]], axis=2)], axis=1).astype(BF16)
    g2p = jnp.pad(gate_g2, ((0, 0), (0, 2 * LANES - d_lg), (0, 0))).astype(BF16)
    lane = jnp.arange(LANES)
    bd = (lane[:, None] // HEAD_N == lane[None, :] // HEAD_N).astype(BF16)
    bd3 = jnp.concatenate([bd, bd, bd], axis=0)
    norm_g3 = norm_g.reshape(depth * 4, 1, d)
    ada_b3 = ada_b.reshape(depth, 1, 6 * d)
    subln3 = diff_subln.reshape(depth, 1, LANES)
    vn3 = chunk_vnorm.reshape(depth, 1, d_c)
    bs_exp = jnp.repeat(jnp.swapaxes(chunk_bs, 1, 2), d_c // n_g, axis=2)
    w_out_b = w_out.astype(BF16)
    rw_p = jnp.pad(router_w, ((0, 0), (0, 0), (0, LANES - n_e)))
    rb_p = jnp.pad(router_b, ((0, 0), (0, LANES - n_e))).reshape(depth, 1, LANES)
    w_gu_b = w_gu.astype(BF16).reshape(depth * n_e, d, -1)
    w_dn_b = w_dn.astype(BF16).reshape(depth * n_e, -1, d)
    b_gu3 = b_gu.reshape(depth * n_e, 1, -1)
    b_dn3 = b_dn.reshape(depth * n_e, 1, d)
    cos_t, sin_t = _rope_tables(s_dec)
    ck4 = cache_k.reshape(b_dec, depth, p_len, d_diff)
    cv4 = cache_v.reshape(b_dec, depth, p_len, d_diff)
    s0_dec = jnp.transpose(state_wkv, (0, 1, 2, 5, 3, 4)).reshape(b_dec, depth, 2, HEAD_N, d_r)
    s0_ctx = jnp.zeros((b_ctx, 2, HEAD_N, d_r), F32)

    cond = jnp.concatenate([c_ctx[None, :], c], axis=0)
    m_rows = -(-cond.shape[0] // 8) * 8
    cond = jnp.pad(cond, ((0, m_rows - cond.shape[0]), (0, 0)))

    x = jnp.concatenate([x_prompt.reshape(t_ctx, d), x_sample.reshape(t_dec, d)], axis=0)
    bm = 512
    ks, vs, ss = [], [], []
    for l in range(depth):
        mods4 = ada_mods(cond, ada_w, ada_b3, l).reshape(m_rows, 6, 1, d)
        z = in_proj(x, mods4, norm_g3, w_in_p, l, t_ctx, s_dec)
        ks.append(z[:t_ctx, (col_q + 1) * d_diff:(col_q + 2) * d_diff])
        vs.append(z[:t_ctx, (col_q + 2) * d_diff:(col_q + 3) * d_diff])

        r, v, kk, w, kd, b, bonus, g = rwkv_prep(z, mu_p, pv, p2, w2bd, a2bd, g2p, bd3, l,
                                                 t_ctx, s_ctx, s_dec, d_r)
        y_c, sf_c = wkv_scan(r, kk, v, w, kd, b, s0_ctx, 0, b_ctx, s_ctx, d_r)
        y_d, _ = wkv_scan(r, kk, v, w, kd, b, s0_dec[:, l], t_ctx, b_dec, s_dec, d_r)
        ss.append(sf_c)
        o_r = rwkv_post(jnp.concatenate([y_c, y_d], axis=1), bonus, g, pv, bd3, l)

        lam_init = 0.8 - 0.6 * math.exp(-0.3 * l)
        lq1, lk1, lq2, lk2 = diff_lam[l].astype(F32)
        lam = (jnp.exp(jnp.sum(lq1 * lk1)) - jnp.exp(jnp.sum(lq2 * lk2)) + lam_init).reshape(1)
        od_c = diff_attn(z, lam, subln3, l, lam_init, 0, b_ctx, s_ctx, d_diff, col_q)
        od_d = diff_attn(z, lam, subln3, l, lam_init, t_ctx, b_dec, s_dec, d_diff, col_q,
                         ctx=(ck4, cv4, cos_t, sin_t))
        o_d = jnp.concatenate([od_c, od_d], axis=0)
        o_c = chunk_mix(z, vn3, chunk_ws, bs_exp, l, d_c, col_q + 3)

        x1, h2, logits = out_proj(x, o_r, o_d, o_c, w_out_b, norm_g3, mods4, rw_p, rb_p, l, t_ctx, s_dec)
        slot_tok, gates, block_e, block_valid, pos = _route(logits[:, :n_e], n_e, bm)
        ys = expert_ffn(h2, slot_tok, block_e, block_valid, w_gu_b, b_gu3, w_dn_b, b_dn3, l, n_e, bm)
        yk = gather_rows(ys, pos.reshape(-1, TOP_K).T.reshape(-1))
        x = moe_combine(x1, yk, gates, norm_g3, mods4, l, t_ctx, s_dec)

    n_hd = cache_k.shape[3]
    new_k = jnp.stack(ks, axis=1).reshape(b_ctx, s_ctx, depth, n_hd, 2, HEAD_N)
    new_k = jnp.transpose(new_k, (0, 2, 1, 3, 4, 5))
    new_v = jnp.stack(vs, axis=1).reshape(b_ctx, s_ctx, depth, n_hd, 2 * HEAD_N)
    new_v = jnp.transpose(new_v, (0, 2, 1, 3, 4))
    st = jnp.stack(ss, axis=1).reshape(b_ctx, depth, 2, HEAD_N, n_h, HEAD_N)
    new_s = jnp.transpose(st, (0, 1, 2, 4, 5, 3))
    y_prompt = x[:t_ctx].reshape(b_ctx, s_ctx, d)
    y_sample = x[t_ctx:].reshape(b_dec, s_dec, d)
    return (y_prompt, y_sample, new_k, new_v, new_s)
```

```python
import functools
import math

import jax
import jax.numpy as jnp
from jax import lax
from jax.experimental import pallas as pl
from jax.experimental.pallas import tpu as pltpu

F32 = jnp.float32
BF16 = jnp.bfloat16

LANES = 128
HEAD_N = 64
NORM_EPS = 1e-6
GN_EPS = 64e-5
ROPE_THETA = 10000.0
GRID_W = 64
TOP_K = 4
SWIGLU_LIMIT = 7.0
SWIGLU_ALPHA = 1.702
CHUNK = 128
VMEM_LIMIT = 56 * 1024 * 1024


def _cparams(sem):
    return pltpu.CompilerParams(dimension_semantics=sem, vmem_limit_bytes=VMEM_LIMIT)


def _split3(x):
    hi = x.astype(BF16)
    r1 = x - hi.astype(F32)
    mid = r1.astype(BF16)
    lo = (r1 - mid.astype(F32)).astype(BF16)
    return hi, mid, lo


def _segsum(x, bd3):
    outs = []
    for j in range(x.shape[1] // LANES):
        hi, mid, lo = _split3(x[:, j * LANES:(j + 1) * LANES])
        lhs = jnp.concatenate([hi, mid, lo], axis=1)
        outs.append(jnp.dot(lhs, bd3, preferred_element_type=F32))
    return outs[0] if len(outs) == 1 else jnp.concatenate(outs, axis=1)


def _rms(x, g):
    return x * lax.rsqrt(jnp.mean(x * x, axis=-1, keepdims=True) + NORM_EPS) * g


def _ada_kernel(c_ref, w_ref, b_ref, o_ref):
    c = c_ref[...]
    s = c * jax.nn.sigmoid(c)
    o_ref[...] = jnp.dot(s.astype(BF16), w_ref[...].astype(BF16),
                         preferred_element_type=F32) + b_ref[...]


def ada_mods(cond, ada_w, ada_b3, l):
    m, d = cond.shape
    n = ada_w.shape[2]
    tn = 1024
    return pl.pallas_call(
        _ada_kernel,
        grid=(n // tn,),
        in_specs=[pl.BlockSpec((m, d), lambda j: (0, 0)),
                  pl.BlockSpec((None, d, tn), lambda j: (l, 0, j)),
                  pl.BlockSpec((None, 1, tn), lambda j: (l, 0, j))],
        out_specs=pl.BlockSpec((m, tn), lambda j: (0, j)),
        out_shape=jax.ShapeDtypeStruct((m, n), F32),
        compiler_params=_cparams(("arbitrary",)),
        name="ada_mods",
    )(cond, ada_w, ada_b3)


def _inproj_kernel(x_ref, g_ref, sh_ref, sc_ref, w_ref, o_ref, h_ref):
    @pl.when(pl.program_id(1) == 0)
    def _():
        h = _rms(x_ref[...], g_ref[...]) * (1.0 + sc_ref[...]) + sh_ref[...]
        h_ref[...] = h.astype(BF16)

    o_ref[...] = jnp.dot(h_ref[...], w_ref[...], preferred_element_type=F32)


def _mod_row(i, tm, t_ctx, s_dec):
    r0 = i * tm
    return jnp.where(r0 < t_ctx, 0, 1 + (r0 - t_ctx) // s_dec)


def in_proj(x, mods4, norm_g3, w_in_p, l, t_ctx, s_dec):
    t, d = x.shape
    n = w_in_p.shape[2]
    tm, tn = math.gcd(1024, t_ctx, s_dec), 512
    mrow = lambda i: _mod_row(i, tm, t_ctx, s_dec)
    return pl.pallas_call(
        _inproj_kernel,
        grid=(t // tm, n // tn),
        in_specs=[pl.BlockSpec((tm, d), lambda i, j: (i, 0)),
                  pl.BlockSpec((None, 1, d), lambda i, j: (4 * l, 0, 0)),
                  pl.BlockSpec((None, None, 1, d), lambda i, j: (mrow(i), 0, 0, 0)),
                  pl.BlockSpec((None, None, 1, d), lambda i, j: (mrow(i), 1, 0, 0)),
                  pl.BlockSpec((None, d, tn), lambda i, j: (l, 0, j))],
        out_specs=pl.BlockSpec((tm, tn), lambda i, j: (i, j)),
        out_shape=jax.ShapeDtypeStruct((t, n), F32),
        scratch_shapes=[pltpu.VMEM((tm, d), BF16)],
        compiler_params=_cparams(("arbitrary", "arbitrary")),
        name="in_proj",
    )(x, norm_g3, mods4, mods4, w_in_p)


def _prep_kernel(z_ref, zp_ref, zn_ref, mu_ref, pv_ref, p2_ref, w2_ref, a2_ref, g2_ref, bd_ref,
                 r_ref, v_ref, kk_ref, w_ref, kd_ref, b_ref, bonus_ref, g_ref, zs_ref,
                 *, tm, t_ctx, s_ctx, s_dec, d_r):
    i = pl.program_id(0)
    r0 = i * tm
    in_ctx = r0 < t_ctx
    pos0 = jnp.where(in_ctx, r0 % s_ctx, (r0 - t_ctx) % s_dec)
    slen = jnp.where(in_ctx, s_ctx, s_dec)
    is_start = pos0 == 0
    is_end = pos0 + tm == slen
    width = z_ref.shape[1]
    rows = lax.broadcasted_iota(jnp.int32, (tm, LANES), 0)
    for j in range(width // LANES):
        sl = slice(j * LANES, (j + 1) * LANES)
        zc = z_ref[:, sl]
        prev_row = jnp.where(is_start, 0.0, zp_ref[7:8, sl])
        next_row = jnp.where(is_end, 0.0, zn_ref[0:1, sl])
        zprev = jnp.where(rows == 0, prev_row, pltpu.roll(zc, 1, axis=0))
        znext = jnp.where(rows == tm - 1, next_row, pltpu.roll(zc, tm - 1, axis=0))
        zs_ref[:, sl] = zc + mu_ref[0:1, sl] * (zprev - zc) + mu_ref[1:2, sl] * (znext - zc)

    bd3 = bd_ref[...]
    r = zs_ref[:, 0:d_r]
    k = zs_ref[:, d_r:2 * d_r]
    v = zs_ref[:, 2 * d_r:3 * d_r]
    o = 3 * d_r
    cw = zs_ref[:, o:o + LANES]
    ca = zs_ref[:, o + LANES:o + 2 * LANES]
    cg = zs_ref[:, o + 2 * LANES:o + 4 * LANES]
    k_k = pv_ref[0:1, :]
    k_a = pv_ref[1:2, :]
    r_k = pv_ref[2:3, :]

    r_ref[...] = r
    v_ref[...] = v
    g_ref[...] = jnp.dot(jax.nn.sigmoid(cg).astype(BF16), g2_ref[...], preferred_element_type=F32)

    kkr = k * k_k
    nrm = jnp.sqrt(_segsum(kkr * kkr, bd3))
    kk = kkr / jnp.maximum(nrm, 1e-12)
    kk_ref[...] = kk

    lw = p2_ref[0:1, :] + jnp.dot(jnp.tanh(cw).astype(BF16), w2_ref[...], preferred_element_type=F32)
    decay = -math.exp(-0.5) * jax.nn.sigmoid(lw)
    a = jax.nn.sigmoid(p2_ref[1:2, :] + jnp.dot(ca.astype(BF16), a2_ref[...], preferred_element_type=F32))
    kd_sum = None
    for d in range(2):
        a_d = a[:, d * d_r:(d + 1) * d_r]
        kd = k * (1.0 + (a_d - 1.0) * k_a)
        w_ref[d] = decay[:, d * d_r:(d + 1) * d_r]
        kd_ref[d] = kd
        b_ref[d] = kk * a_d
        kd_sum = kd if kd_sum is None else kd_sum + kd
    bonus_ref[...] = _segsum(r * kd_sum * r_k, bd3) * v


def rwkv_prep(z, mu_p, pv, p2, w2bd, a2bd, g2p, bd3, l, t_ctx, s_ctx, s_dec, d_r):
    t = z.shape[0]
    tm = 128
    wz = mu_p.shape[2]
    nb8 = t // 8
    kern = functools.partial(_prep_kernel, tm=tm, t_ctx=t_ctx, s_ctx=s_ctx, s_dec=s_dec, d_r=d_r)
    row = pl.BlockSpec((tm, d_r), lambda i: (i, 0))
    row2 = pl.BlockSpec((2, tm, d_r), lambda i: (0, i, 0))
    o1 = jax.ShapeDtypeStruct((t, d_r), F32)
    o2 = jax.ShapeDtypeStruct((2, t, d_r), F32)
    return pl.pallas_call(
        kern,
        grid=(t // tm,),
        in_specs=[pl.BlockSpec((tm, wz), lambda i: (i, 0)),
                  pl.BlockSpec((8, wz), lambda i: (jnp.maximum(i * (tm // 8) - 1, 0), 0)),
                  pl.BlockSpec((8, wz), lambda i: (jnp.minimum((i + 1) * (tm // 8), nb8 - 1), 0)),
                  pl.BlockSpec((None, 8, wz), lambda i: (l, 0, 0)),
                  pl.BlockSpec((None, 8, d_r), lambda i: (l, 0, 0)),
                  pl.BlockSpec((None, 8, 2 * d_r), lambda i: (l, 0, 0)),
                  pl.BlockSpec((None, LANES, 2 * d_r), lambda i: (l, 0, 0)),
                  pl.BlockSpec((None, LANES, 2 * d_r), lambda i: (l, 0, 0)),
                  pl.BlockSpec((None, 2 * LANES, d_r), lambda i: (l, 0, 0)),
                  pl.BlockSpec((3 * LANES, LANES), lambda i: (0, 0))],
        out_specs=[row, row, row, row2, row2, row2, row, row],
        out_shape=[o1, o1, o1, o2, o2, o2, o1, o1],
        scratch_shapes=[pltpu.VMEM((tm, wz), F32)],
        compiler_params=_cparams(("arbitrary",)),
        name="rwkv_prep",
    )(z, z, z, mu_p, pv, p2, w2bd, a2bd, g2p, bd3)


def _split2(x):
    hi = x.astype(BF16)
    lo = (x - hi.astype(F32)).astype(BF16)
    return hi, lo


def _lhs3(a, axis):
    ah, al = _split2(a)
    return jnp.concatenate([ah, ah, al], axis=axis)


def _rhs3(b, axis):
    bh, bl = _split2(b)
    return jnp.concatenate([bh, bl, bh], axis=axis)


def _dot3(l3, r3):
    return jnp.dot(l3, r3, preferred_element_type=F32)


def _dot3_nt(l3, r3):
    return lax.dot_general(l3, r3, (((1,), (1,)), ((), ())), preferred_element_type=F32)


def _dot3_tn(l3, r3):
    return lax.dot_general(l3, r3, (((0,), (0,)), ((), ())), preferred_element_type=F32)


def _mm1(a, b):
    return jnp.dot(a.astype(BF16), b.astype(BF16), preferred_element_type=F32)


SCAN_C = 64


def _scan_kernel(r_ref, kk_ref, v_ref, lw_ref, kd_ref, b_ref, s0_ref,
                 y_ref, sf_ref, s_ref, kr_s, t_s, g4_s, gv_s, kb_s, gc_s, *, tb, nt, nlt):
    c_len = SCAN_C
    fwd = pl.program_id(1) == 0
    tstep = pl.program_id(3)
    lane = lax.broadcasted_iota(jnp.int32, (c_len, LANES), 1)
    row = lax.broadcasted_iota(jnp.int32, (c_len, LANES), 0)
    colm = lane & (HEAD_N - 1)
    ahead = (row - colm) * jnp.where(fwd, 1, -1)
    strict = ahead > 0
    incl = ahead >= 0
    h0 = lane < HEAD_N
    lane2 = lax.broadcasted_iota(jnp.int32, (2 * c_len, LANES), 1)
    row2 = lax.broadcasted_iota(jnp.int32, (2 * c_len, LANES), 0)
    bdmask = (lane2 < HEAD_N) == (row2 < HEAD_N)
    eye = lane2 == row2
    m_incl = jnp.where(jnp.logical_and(incl, h0), 1.0, 0.0).astype(BF16)
    m_incl3 = jnp.concatenate([m_incl, m_incl, m_incl], axis=1)
    zpad = jnp.zeros((c_len, LANES), BF16)

    def two_heads(x):
        return jnp.concatenate([jnp.where(h0, x, 0.0), jnp.where(h0, 0.0, x)], axis=0)

    @pl.when(tstep == 0)
    def _():
        for j in range(nlt):
            s_ref[j] = two_heads(s0_ref[:, j * LANES:(j + 1) * LANES])

    n_ch = tb // c_len

    def chunk_rows(ci):
        cc = jnp.where(fwd, ci, n_ch - 1 - ci)
        return pl.ds(pl.multiple_of(cc * c_len, c_len), c_len)

    tiles = range(nlt)
    lanes_of = [slice(j * LANES, (j + 1) * LANES) for j in tiles]
    eye_f = jnp.where(eye, 1.0, 0.0)
    for ci in range(n_ch):
        rows = chunk_rows(ci)
        lws = [lw_ref[rows, sl] for sl in lanes_of]
        lgs = []
        for lw in lws:
            hi, mid, lo = _split3(lw)
            lgs.append(jnp.dot(m_incl3, jnp.concatenate([hi, zpad, mid, zpad, lo, zpad], axis=0),
                               preferred_element_type=F32))
        lasts = [jnp.where(fwd, lg[c_len - 1:c_len, :], lg[0:1, :]) for lg in lgs]
        kr3s = [_lhs3(jnp.concatenate([kk_ref[rows, sl] * jnp.exp(lg - lw), r_ref[rows, sl] * jnp.exp(lg)],
                                      axis=0), 1)
                for sl, lg, lw in zip(lanes_of, lgs, lws)]
        ens = [jnp.exp(-lg) for lg in lgs]
        gabs = [_dot3_nt(kr3, jnp.concatenate([_rhs3(two_heads(kd_ref[rows, sl] * en), 1),
                                               _rhs3(two_heads(b_ref[rows, sl] * en), 1)], axis=0))
                for kr3, sl, en in zip(kr3s, lanes_of, ens)]
        gas = [gab[:, :LANES] for gab in gabs]
        gbs = [gab[:, LANES:] for gab in gabs]
        ps = [-two_heads(jnp.where(strict, gb[:c_len], 0.0)) for gb in gbs]
        accs = [eye_f + p for p in ps]
        ps = [_mm1(p, p) for p in ps]
        span = 4
        while span < c_len:
            both = [_mm1(jnp.concatenate([p, acc], axis=0), p) for p, acc in zip(ps, accs)]
            accs = [acc + b2[2 * c_len:] for acc, b2 in zip(accs, both)]
            ps = [b2[:2 * c_len] for b2 in both]
            span *= 2
        accs = [acc + _mm1(acc, p) for acc, p in zip(accs, ps)]
        gvs = [_dot3(_lhs3(jnp.concatenate([jnp.where(strict, ga[:c_len], 0.0),
                                            jnp.where(incl, ga[c_len:], 0.0)], axis=0), 1),
                     _rhs3(two_heads(v_ref[rows, sl]), 0))
               for ga, sl in zip(gas, lanes_of)]
        for j in tiles:
            q = ci * nlt + j
            ec = jnp.exp(lasts[j] - lgs[j])
            kr_s[q] = kr3s[j]
            t_s[q] = _lhs3(accs[j][:c_len] + accs[j][c_len:], 1)
            g4_s[q] = _lhs3(jnp.where(incl, gbs[j][c_len:], 0.0), 1)
            gv_s[q] = gvs[j]
            kb_s[q] = _lhs3(jnp.concatenate([kd_ref[rows, lanes_of[j]] * ec, b_ref[rows, lanes_of[j]] * ec],
                                            axis=0), 0)
            gcol = jnp.transpose(jnp.broadcast_to(jnp.exp(lasts[j]), (2 * c_len, LANES)))
            gc_s[q] = jnp.where(bdmask, gcol, 0.0)

    for ci in range(n_ch):
        rows = chunk_rows(ci)
        qs = [ci * nlt + j for j in tiles]
        bdss = [s_ref[j] for j in tiles]
        xrs = [_dot3(kr_s[q], _rhs3(bds, 0)) for q, bds in zip(qs, bdss)]
        us = [_dot3(t_s[q], _rhs3(two_heads(xr[:c_len] + gv_s[q, :c_len]), 0)) for q, xr in zip(qs, xrs)]
        news = [_dot3_tn(kb_s[q], _rhs3(jnp.concatenate([v_ref[rows, sl], -u], axis=0), 0))
                for q, sl, u in zip(qs, lanes_of, us)]
        for j in tiles:
            s_ref[j] = jnp.where(bdmask, news[j], 0.0) + bdss[j] * gc_s[qs[j]]
        for j in tiles:
            y_ref[rows, lanes_of[j]] = (xrs[j][c_len:] + gv_s[qs[j], c_len:]
                                        - _dot3(g4_s[qs[j]], _rhs3(two_heads(us[j]), 0)))

    @pl.when(tstep == nt - 1)
    def _():
        for j in range(nlt):
            bds = s_ref[j]
            sf_ref[:, j * LANES:(j + 1) * LANES] = bds[:HEAD_N] + bds[HEAD_N:]


def wkv_scan(r, kk, v, lw, kd, b, s0, row0, n_seq, s_len, d_r):
    tb = min(256, s_len)
    nt = s_len // tb
    blk0 = row0 // tb
    nlt = 8
    wl = nlt * LANES
    nq = (tb // SCAN_C) * nlt

    def tblk(bi, d, c):
        return blk0 + bi * nt + c + d * (nt - 1 - 2 * c)

    shared = pl.BlockSpec((tb, wl), lambda bi, d, hp, c: (tblk(bi, d, c), hp))
    perdir = pl.BlockSpec((None, tb, wl), lambda bi, d, hp, c: (d, tblk(bi, d, c), hp))
    state = pl.BlockSpec((None, None, HEAD_N, wl), lambda bi, d, hp, c: (bi, d, 0, hp))
    kern = functools.partial(_scan_kernel, tb=tb, nt=nt, nlt=nlt)
    return pl.pallas_call(
        kern,
        grid=(n_seq, 2, d_r // wl, nt),
        in_specs=[shared, shared, shared, perdir, perdir, perdir, state],
        out_specs=[pl.BlockSpec((None, tb, wl),
                                lambda bi, d, hp, c: (d, bi * nt + c + d * (nt - 1 - 2 * c), hp)),
                   state],
        out_shape=[jax.ShapeDtypeStruct((2, n_seq * s_len, d_r), F32),
                   jax.ShapeDtypeStruct((n_seq, 2, HEAD_N, d_r), F32)],
        scratch_shapes=[pltpu.VMEM((nlt, 2 * HEAD_N, LANES), F32),
                        pltpu.VMEM((nq, 2 * SCAN_C, 3 * LANES), BF16),
                        pltpu.VMEM((nq, SCAN_C, 3 * LANES), BF16),
                        pltpu.VMEM((nq, SCAN_C, 3 * LANES), BF16),
                        pltpu.VMEM((nq, 2 * SCAN_C, LANES), F32),
                        pltpu.VMEM((nq, 6 * SCAN_C, LANES), BF16),
                        pltpu.VMEM((nq, 2 * HEAD_N, LANES), F32)],
        compiler_params=_cparams(("arbitrary", "arbitrary", "arbitrary", "arbitrary")),
        name="wkv_scan",
    )(r, kk, v, lw, kd, b, s0)


def _post_kernel(y_ref, bonus_ref, g_ref, pv_ref, bd_ref, o_ref):
    bd3 = bd_ref[...]
    y = y_ref[0] + y_ref[1]
    mu = _segsum(y, bd3) * (1.0 / HEAD_N)
    yc = y - mu
    var = _segsum(yc * yc, bd3) * (1.0 / HEAD_N)
    yn = yc * lax.rsqrt(var + GN_EPS) * pv_ref[3:4, :] + pv_ref[4:5, :]
    o_ref[...] = (yn + bonus_ref[...]) * g_ref[...]


def rwkv_post(y, bonus, g, pv, bd3, l):
    t, d_r = bonus.shape
    tm = 256
    row = pl.BlockSpec((tm, d_r), lambda i: (i, 0))
    return pl.pallas_call(
        _post_kernel,
        grid=(t // tm,),
        in_specs=[pl.BlockSpec((2, tm, d_r), lambda i: (0, i, 0)), row, row,
                  pl.BlockSpec((None, 8, d_r), lambda i: (l, 0, 0)),
                  pl.BlockSpec((3 * LANES, LANES), lambda i: (0, 0))],
        out_specs=row,
        out_shape=jax.ShapeDtypeStruct((t, d_r), F32),
        compiler_params=_cparams(("arbitrary",)),
        name="rwkv_post",
    )(y, bonus, g, pv, bd3)


def _rope(x, cos, sin):
    lane = lax.broadcasted_iota(jnp.int32, (x.shape[0], LANES), 1)
    first = (lane & 31) < 16
    outs = []
    for j in range(x.shape[1] // LANES):
        xc = x[:, j * LANES:(j + 1) * LANES]
        partner = jnp.where(first, pltpu.roll(xc, LANES - 16, axis=1), pltpu.roll(xc, 16, axis=1))
        outs.append(xc * cos + partner * sin)
    return outs[0] if len(outs) == 1 else jnp.concatenate(outs, axis=1)


def _attn_kernel(lam_ref, q_ref, k_ref, v_ref, *rest, use_ctx, n_heads, out_scale):
    if use_ctx:
        (ck_ref, cv_ref, cosq_ref, sinq_ref, cosk_ref, sink_ref, g_ref,
         o_ref, kb_ref, vb_ref, ckb_ref, cvb_ref) = rest
    else:
        g_ref, o_ref, kb_ref, vb_ref = rest
    qi = pl.program_id(1)

    @pl.when(qi == 0)
    def _():
        k = k_ref[...]
        if use_ctx:
            k = _rope(k, cosk_ref[...], sink_ref[...])
            ckb_ref[...] = ck_ref[...].astype(BF16)
            cvb_ref[...] = cv_ref[...].astype(BF16)
        kb_ref[...] = k.astype(BF16)
        vb_ref[...] = v_ref[...].astype(BF16)

    lam = lam_ref[0]
    q = q_ref[...]
    if use_ctx:
        q = _rope(q, cosq_ref[...], sinq_ref[...])
    tq = q.shape[0]
    lane = lax.broadcasted_iota(jnp.int32, (tq, LANES), 1)
    scale = HEAD_N ** -0.5
    dn = (((1,), (1,)), ((), ()))
    for h in range(n_heads):
        sl = slice(h * LANES, (h + 1) * LANES)
        qh = q[:, sl] * scale
        kh = kb_ref[:, sl]
        outs = []
        for comp in range(2):
            qc = jnp.where((lane < HEAD_N) == (comp == 0), qh, 0.0).astype(BF16)
            s_a = lax.dot_general(qc, kh, dn, preferred_element_type=F32)
            m = jnp.max(s_a, axis=-1, keepdims=True)
            if use_ctx:
                s_b = lax.dot_general(qc, ckb_ref[:, sl], dn, preferred_element_type=F32)
                m = jnp.maximum(m, jnp.max(s_b, axis=-1, keepdims=True))
            e_a = jnp.exp(s_a - m)
            den = jnp.sum(e_a, axis=-1, keepdims=True)
            acc = jnp.dot(e_a.astype(BF16), vb_ref[:, sl], preferred_element_type=F32)
            if use_ctx:
                e_b = jnp.exp(s_b - m)
                den = den + jnp.sum(e_b, axis=-1, keepdims=True)
                acc = acc + jnp.dot(e_b.astype(BF16), cvb_ref[:, sl], preferred_element_type=F32)
            outs.append(acc / den)
        o = outs[0] - lam * outs[1]
        o_ref[:, sl] = _rms(o, g_ref[...]) * out_scale


def diff_attn(z, lam, subln3, l, lam_init, row0, n_seq, s_len, d_diff, col0, ctx=None):
    tq = 256
    nq = s_len // tq
    n_heads = d_diff // LANES
    qb0 = row0 // tq
    sb0 = row0 // s_len
    use_ctx = ctx is not None
    in_specs = [pl.BlockSpec(memory_space=pltpu.SMEM),
                pl.BlockSpec((tq, d_diff), lambda bi, qi: (qb0 + bi * nq + qi, col0)),
                pl.BlockSpec((s_len, d_diff), lambda bi, qi: (sb0 + bi, col0 + 1)),
                pl.BlockSpec((s_len, d_diff), lambda bi, qi: (sb0 + bi, col0 + 2))]
    args = [lam, z, z, z]
    scratch = [pltpu.VMEM((s_len, d_diff), BF16), pltpu.VMEM((s_len, d_diff), BF16)]
    if use_ctx:
        cache_k, cache_v, cos, sin = ctx
        p_len = cache_k.shape[2]
        cspec = pl.BlockSpec((None, None, p_len, d_diff), lambda bi, qi: (bi, l, 0, 0))
        in_specs += [cspec, cspec,
                     pl.BlockSpec((tq, LANES), lambda bi, qi: (qi, 0)),
                     pl.BlockSpec((tq, LANES), lambda bi, qi: (qi, 0)),
                     pl.BlockSpec((s_len, LANES), lambda bi, qi: (0, 0)),
                     pl.BlockSpec((s_len, LANES), lambda bi, qi: (0, 0))]
        args += [cache_k, cache_v, cos, sin, cos, sin]
        scratch += [pltpu.VMEM((p_len, d_diff), BF16), pltpu.VMEM((p_len, d_diff), BF16)]
    in_specs.append(pl.BlockSpec((None, 1, LANES), lambda bi, qi: (l, 0, 0)))
    args.append(subln3)
    kern = functools.partial(_attn_kernel, use_ctx=use_ctx, n_heads=n_heads,
                             out_scale=1.0 - lam_init)
    return pl.pallas_call(
        kern,
        grid=(n_seq, nq),
        in_specs=in_specs,
        out_specs=pl.BlockSpec((tq, d_diff), lambda bi, qi: (bi * nq + qi, 0)),
        out_shape=jax.ShapeDtypeStruct((n_seq * s_len, d_diff), F32),
        scratch_shapes=scratch,
        compiler_params=_cparams(("arbitrary", "arbitrary")),
        name="diff_attn_ctx" if use_ctx else "diff_attn",
    )(*args)


def _chunk_kernel(u_ref, g_ref, vn_ref, ws_ref, bs_ref, o_ref):
    u = jax.nn.gelu(u_ref[...], approximate=True)
    vv = _rms(jax.nn.gelu(g_ref[...], approximate=True), vn_ref[...])
    for gi in range(ws_ref.shape[0]):
        sl = slice(gi * LANES, (gi + 1) * LANES)
        s = jnp.dot(ws_ref[gi].astype(BF16), vv[:, sl].astype(BF16), preferred_element_type=F32)
        o_ref[:, sl] = u[:, sl] * (s + bs_ref[:, sl])


def chunk_mix(z, vn3, chunk_ws, bs_exp, l, d_c, col_u):
    t = z.shape[0]
    n_g = chunk_ws.shape[1]
    return pl.pallas_call(
        _chunk_kernel,
        grid=(t // CHUNK,),
        in_specs=[pl.BlockSpec((CHUNK, d_c), lambda i: (i, col_u)),
                  pl.BlockSpec((CHUNK, d_c), lambda i: (i, col_u + 1)),
                  pl.BlockSpec((None, 1, d_c), lambda i: (l, 0, 0)),
                  pl.BlockSpec((None, n_g, CHUNK, CHUNK), lambda i: (l, 0, 0, 0)),
                  pl.BlockSpec((None, CHUNK, d_c), lambda i: (l, 0, 0))],
        out_specs=pl.BlockSpec((CHUNK, d_c), lambda i: (i, 0)),
        out_shape=jax.ShapeDtypeStruct((t, d_c), F32),
        compiler_params=_cparams(("arbitrary",)),
        name="chunk_mix",
    )(z, z, vn3, chunk_ws, bs_exp)


def _outproj_kernel(x_ref, or_ref, od_ref, oc_ref, wr_ref, wd_ref, wc_ref, g1n_ref, g2n_ref,
                    gate_ref, sh_ref, sc_ref, rw_ref, rb_ref, x1_ref, h2_ref, lg_ref):
    mix = jnp.dot(or_ref[...].astype(BF16), wr_ref[...], preferred_element_type=F32)
    mix = mix + jnp.dot(od_ref[...].astype(BF16), wd_ref[...], preferred_element_type=F32)
    mix = mix + jnp.dot(oc_ref[...].astype(BF16), wc_ref[...], preferred_element_type=F32)
    x1 = x_ref[...] + gate_ref[...] * _rms(mix, g1n_ref[...])
    x1_ref[...] = x1
    h2 = _rms(x1, g2n_ref[...]) * (1.0 + sc_ref[...]) + sh_ref[...]
    half = h2.shape[1] // 2
    bits = lax.bitcast_convert_type(h2.astype(BF16).astype(F32), jnp.uint32)
    h2_ref[...] = (bits[:, :half] >> 16) | bits[:, half:]
    lg_ref[...] = _dot3(_lhs3(h2, 1), _rhs3(rw_ref[...], 0)) + rb_ref[...]


def out_proj(x, o_r, o_d, o_c, w_out_b, norm_g3, mods4, rw_p, rb_p, l, t_ctx, s_dec):
    t, d = x.shape
    d_r, d_d, d_c = o_r.shape[1], o_d.shape[1], o_c.shape[1]
    tm = math.gcd(256, t_ctx, s_dec)
    mrow = lambda i: _mod_row(i, tm, t_ctx, s_dec)
    nrm = lambda k: pl.BlockSpec((None, 1, d), lambda i: (4 * l + k, 0, 0))
    mod = lambda k: pl.BlockSpec((None, None, 1, d), lambda i: (mrow(i), k, 0, 0))
    row = lambda w: pl.BlockSpec((tm, w), lambda i: (i, 0))
    nrb = d_r // d_d
    return pl.pallas_call(
        _outproj_kernel,
        grid=(t // tm,),
        in_specs=[row(d), row(d_r), row(d_d), row(d_c),
                  pl.BlockSpec((None, d_r, d), lambda i: (l, 0, 0)),
                  pl.BlockSpec((None, d_d, d), lambda i: (l, nrb, 0)),
                  pl.BlockSpec((None, d_c, d), lambda i: (l, nrb + 1, 0)),
                  nrm(1), nrm(2), mod(2), mod(3), mod(4),
                  pl.BlockSpec((None, d, LANES), lambda i: (l, 0, 0)),
                  pl.BlockSpec((None, 1, LANES), lambda i: (l, 0, 0))],
        out_specs=[row(d), row(d // 2), row(LANES)],
        out_shape=[jax.ShapeDtypeStruct((t, d), F32), jax.ShapeDtypeStruct((t, d // 2), jnp.uint32),
                   jax.ShapeDtypeStruct((t, LANES), F32)],
        compiler_params=_cparams(("arbitrary",)),
        name="out_proj",
    )(x, o_r, o_d, o_c, w_out_b, w_out_b, w_out_b, norm_g3, norm_g3, mods4, mods4, mods4, rw_p, rb_p)


def _gather_kernel(idx_ref, src_ref, o_ref, sem, *, bm):
    def row_copy(r, src_row):
        return pltpu.make_async_copy(src_ref.at[pl.ds(src_row, 1), :], o_ref.at[pl.ds(r, 1), :], sem)

    def issue(i, carry):
        r = 2 * i
        row_copy(r, idx_ref[0, r]).start(priority=0)
        row_copy(r + 1, idx_ref[0, r + 1]).start(priority=1)
        return carry

    lax.fori_loop(0, bm // 2, issue, 0)
    pltpu.make_async_copy(o_ref, o_ref, sem).wait()


def gather_rows(src, idx, bm=512):
    n, d = src.shape
    m = idx.shape[0]
    return pl.pallas_call(
        functools.partial(_gather_kernel, bm=bm),
        grid=(m // bm,),
        in_specs=[pl.BlockSpec((None, 1, bm), lambda i: (i, 0, 0), memory_space=pltpu.SMEM),
                  pl.BlockSpec(memory_space=pl.ANY)],
        out_specs=pl.BlockSpec((bm, d), lambda i: (i, 0)),
        out_shape=jax.ShapeDtypeStruct((m, d), src.dtype),
        scratch_shapes=[pltpu.SemaphoreType.DMA],
        compiler_params=_cparams(("arbitrary",)),
        name="gather_rows",
    )(idx.reshape(m // bm, 1, bm), src)


def _expert_kernel(be_ref, bv_ref, idx_ref, nidx_ref, h2_ref, wg_ref, wu_ref, bg_ref, bu_ref, wd_ref,
                   bdn_ref, o_ref, xb_ref, xg_ref, act_ref, sem, *, nf, nb, bm):
    i = pl.program_id(0)
    f = pl.program_id(1)
    slot = i % 2
    per_step = bm // nf

    def row_copy(src_idx_ref, row, s):
        return pltpu.make_async_copy(h2_ref.at[pl.ds(src_idx_ref[0, row], 1), :],
                                     xg_ref.at[s, pl.ds(row, 1), :], sem.at[s])

    def block_wait(s):
        pltpu.make_async_copy(xg_ref.at[s], xg_ref.at[s], sem.at[s]).wait()

    @pl.when(jnp.logical_and(i == 0, f == 0))
    def _():
        def issue(j, carry):
            row_copy(idx_ref, 2 * j, 0).start(priority=0)
            row_copy(idx_ref, 2 * j + 1, 0).start(priority=1)
            return carry
        lax.fori_loop(0, bm // 2, issue, 0)

    @pl.when(f == 0)
    def _():
        block_wait(slot)
        p = xg_ref[slot]
        half = p.shape[1]
        xb_ref[:, :half] = lax.bitcast_convert_type(p << 16, F32).astype(BF16)
        xb_ref[:, half:] = lax.bitcast_convert_type(p & jnp.uint32(0xFFFF0000), F32).astype(BF16)
        o_ref[...] = jnp.zeros_like(o_ref)

    row0 = f * per_step
    thirds = [(per_step * g) // 3 for g in range(4)]

    def issue_rows(g):
        for r in range(thirds[g], thirds[g + 1]):
            row_copy(nidx_ref, row0 + r, 1 - slot).start(priority=r % 2)

    @pl.when(bv_ref[i] > 0)
    def _():
        issue_rows(0)

    @pl.when(bv_ref[i] > 0)
    def _():
        issue_rows(1)
        xb = xb_ref[...]
        gt = jnp.dot(xb, wg_ref[...], preferred_element_type=F32) + bg_ref[...]
        up = jnp.dot(xb, wu_ref[...], preferred_element_type=F32) + bu_ref[...]
        gt = jnp.minimum(gt, SWIGLU_LIMIT)
        up = jnp.clip(up, -SWIGLU_LIMIT, SWIGLU_LIMIT)
        act_ref[...] = ((up + 1.0) * gt * jax.nn.sigmoid(SWIGLU_ALPHA * gt)).astype(BF16)

    @pl.when(bv_ref[i] > 0)
    def _():
        issue_rows(2)
        o_ref[...] += jnp.dot(act_ref[...], wd_ref[...], preferred_element_type=F32)

    @pl.when(bv_ref[i] <= 0)
    def _():
        def issue(j, carry):
            row_copy(nidx_ref, row0 + 2 * j, 1 - slot).start(priority=0)
            row_copy(nidx_ref, row0 + 2 * j + 1, 1 - slot).start(priority=1)
            return carry
        lax.fori_loop(0, per_step // 2, issue, 0)

    @pl.when(f == nf - 1)
    def _():
        o_ref[...] = o_ref[...] + bdn_ref[...]

    @pl.when(jnp.logical_and(i == nb - 1, f == nf - 1))
    def _():
        block_wait(1 - slot)


def expert_ffn(h2p, slot_tok, block_e, block_valid, w_gu_b, b_gu3, w_dn_b, b_dn3, l, n_e, bm):
    d_half = h2p.shape[1]
    n_slots = slot_tok.shape[0]
    d = 2 * d_half
    ff = w_dn_b.shape[1]
    ft = 512
    nf = ff // ft
    nb = n_slots // bm
    e0 = l * n_e
    idx3 = slot_tok.reshape(nb, 1, bm)

    def fsel(i, f, bv):
        return jnp.where(bv[i] > 0, f, nf - 1)

    return pl.pallas_call(
        functools.partial(_expert_kernel, nf=nf, nb=nb, bm=bm),
        grid_spec=pltpu.PrefetchScalarGridSpec(
            num_scalar_prefetch=2,
            grid=(nb, nf),
            in_specs=[pl.BlockSpec((None, 1, bm), lambda i, f, be, bv: (i, 0, 0), memory_space=pltpu.SMEM),
                      pl.BlockSpec((None, 1, bm), lambda i, f, be, bv: (jnp.minimum(i + 1, nb - 1), 0, 0),
                                   memory_space=pltpu.SMEM),
                      pl.BlockSpec(memory_space=pl.ANY),
                      pl.BlockSpec((None, d, ft), lambda i, f, be, bv: (e0 + be[i], 0, fsel(i, f, bv))),
                      pl.BlockSpec((None, d, ft), lambda i, f, be, bv: (e0 + be[i], 0, nf + fsel(i, f, bv))),
                      pl.BlockSpec((None, 1, ft), lambda i, f, be, bv: (e0 + be[i], 0, fsel(i, f, bv))),
                      pl.BlockSpec((None, 1, ft), lambda i, f, be, bv: (e0 + be[i], 0, nf + fsel(i, f, bv))),
                      pl.BlockSpec((None, ft, d), lambda i, f, be, bv: (e0 + be[i], fsel(i, f, bv), 0)),
                      pl.BlockSpec((None, 1, d), lambda i, f, be, bv: (e0 + be[i], 0, 0))],
            out_specs=pl.BlockSpec((bm, d), lambda i, f, be, bv: (i, 0)),
            scratch_shapes=[pltpu.VMEM((bm, d), BF16),
                            pltpu.VMEM((2, bm, d_half), jnp.uint32),
                            pltpu.VMEM((bm, ft), BF16),
                            pltpu.SemaphoreType.DMA((2,))]),
        out_shape=jax.ShapeDtypeStruct((n_slots, d), F32),
        compiler_params=_cparams(("arbitrary", "arbitrary")),
        name="expert_ffn",
    )(block_e, block_valid, idx3, idx3, h2p, w_gu_b, w_gu_b, b_gu3, b_gu3, w_dn_b, b_dn3)


def _combine_kernel(x_ref, *rest):
    y_refs = rest[:TOP_K]
    rg_ref, gn_ref, gate_ref, o_ref = rest[TOP_K:]
    f = y_refs[0][...] * rg_ref[:, 0:1]
    for k in range(1, TOP_K):
        f = f + y_refs[k][...] * rg_ref[:, k:k + 1]
    o_ref[...] = x_ref[...] + gate_ref[...] * _rms(f, gn_ref[...])


def moe_combine(x1, yk, gates, norm_g3, mods4, l, t_ctx, s_dec):
    t, d = x1.shape
    tm = 256
    nblk = t // tm
    mrow = lambda i: _mod_row(i, tm, t_ctx, s_dec)
    y_specs = [pl.BlockSpec((tm, d), functools.partial(lambda i, k: (k * nblk + i, 0), k=k))
               for k in range(TOP_K)]
    return pl.pallas_call(
        _combine_kernel,
        grid=(nblk,),
        in_specs=[pl.BlockSpec((tm, d), lambda i: (i, 0)),
                  *y_specs,
                  pl.BlockSpec((tm, TOP_K), lambda i: (i, 0)),
                  pl.BlockSpec((None, 1, d), lambda i: (4 * l + 3, 0, 0)),
                  pl.BlockSpec((None, None, 1, d), lambda i: (mrow(i), 5, 0, 0))],
        out_specs=pl.BlockSpec((tm, d), lambda i: (i, 0)),
        out_shape=jax.ShapeDtypeStruct((t, d), F32),
        compiler_params=_cparams(("arbitrary",)),
        name="moe_combine",
    )(x1, *([yk] * TOP_K), gates, norm_g3, mods4)


def _route(logits, n_e, bm):
    t = logits.shape[0]
    top_v, top_i = lax.top_k(logits, TOP_K)
    gates = jax.nn.softmax(top_v, axis=-1)
    tk = t * TOP_K
    flat_e = top_i.reshape(-1).astype(jnp.int32)
    eids = jnp.arange(n_e, dtype=jnp.int32)
    iota = jnp.arange(tk, dtype=jnp.int32)
    sorted_e, order = lax.sort_key_val(flat_e, iota)
    counts = jnp.sum((flat_e[:, None] == eids[None, :]).astype(jnp.int32), axis=0)
    padded = ((counts + bm - 1) // bm) * bm
    pad_end = jnp.cumsum(padded)
    pad_start = pad_end - padded
    start = jnp.cumsum(counts) - counts
    shift = pad_start - start
    dest = iota + jnp.sum(jnp.where(sorted_e[:, None] == eids[None, :], shift[None, :], 0), axis=1)
    _, pos = lax.sort_key_val(order, dest)
    nb = (tk + n_e * (bm - 1) + bm - 1) // bm
    starts = jnp.arange(nb, dtype=jnp.int32) * bm
    block_e = jnp.minimum(jnp.sum((pad_end[None, :] <= starts[:, None]).astype(jnp.int32), axis=1), n_e - 1)
    block_valid = (starts < pad_end[-1]).astype(jnp.int32)
    src = starts[:, None] + jnp.arange(bm, dtype=jnp.int32)[None, :] - shift[block_e][:, None]
    live = jnp.logical_and(src < (start + counts)[block_e][:, None], block_valid[:, None] > 0)
    slot_tok = jnp.where(live, order[jnp.clip(src, 0, tk - 1)] // TOP_K, 0).reshape(-1)
    last_e = jnp.max(jnp.where(counts > 0, eids, 0))
    block_e = jnp.where(block_valid > 0, block_e, last_e)
    return slot_tok, gates, block_e, block_valid, pos


def _rope_tables(s_len):
    rows = s_len // GRID_W
    row = jnp.repeat(jnp.arange(rows), GRID_W)
    col = jnp.tile(jnp.arange(GRID_W), rows)
    half = HEAD_N // 2
    inv_freq = 1.0 / (ROPE_THETA ** (jnp.arange(0, half, 2, dtype=F32) / half))
    ang = jnp.stack([row, col], axis=-1).astype(F32)[..., None] * inv_freq
    cos, sin = jnp.cos(ang), jnp.sin(ang)
    cos64 = jnp.concatenate([cos[:, 0], cos[:, 0], cos[:, 1], cos[:, 1]], axis=-1)
    sin64 = jnp.concatenate([-sin[:, 0], sin[:, 0], -sin[:, 1], sin[:, 1]], axis=-1)
    return jnp.tile(cos64, (1, 2)), jnp.tile(sin64, (1, 2))


def kernel(x_prompt, x_sample, c, cache_k, cache_v, state_wkv, c_ctx, ada_w, ada_b, norm_g, w_in, shift_mu, decay_w0, decay_w2, iclr_a0, iclr_a2, gate_g2, k_k, k_a, r_k, ln_x_w, ln_x_b, diff_lam, diff_subln, chunk_vnorm, chunk_ws, chunk_bs, w_out, router_w, router_b, w_gu, b_gu, w_dn, b_dn):
    b_ctx, s_ctx, d = x_prompt.shape
    b_dec, s_dec, _ = x_sample.shape
    depth = ada_w.shape[0]
    t_ctx, t_dec = b_ctx * s_ctx, b_dec * s_dec
    d_r = k_k.shape[1]
    n_h = d_r // HEAD_N
    d_rin = shift_mu.shape[2]
    d_diff = diff_subln.shape[1] * (cache_k.shape[3])
    n_g = chunk_ws.shape[1]
    d_c = n_g * chunk_ws.shape[2]
    n_e = router_w.shape[2]
    d_lw, d_la, d_lg = decay_w2.shape[2], iclr_a2.shape[2], gate_g2.shape[1]
    p_len = cache_k.shape[2]
    assert d_lw == HEAD_N and d_la == HEAD_N and d_lg <= 2 * LANES and d_diff == d_c
    wz = 3 * d_r + 4 * LANES
    col_q = wz // d_diff
    assert wz % d_diff == 0 and d_rin <= wz

    padc = wz - d_rin
    w_in_p = jnp.concatenate([w_in[:, :, :d_rin], jnp.zeros((depth, d, padc), F32), w_in[:, :, d_rin:]],
                             axis=2).astype(BF16)
    mu_p = jnp.pad(shift_mu, ((0, 0), (0, 6), (0, padc)))
    pv = jnp.stack([k_k, k_a, r_k.reshape(depth, d_r), ln_x_w, ln_x_b], axis=1)
    pv = jnp.pad(pv, ((0, 0), (0, 3), (0, 0)))
    p2 = jnp.stack([decay_w0.reshape(depth, 2 * d_r), iclr_a0.reshape(depth, 2 * d_r)], axis=1)
    p2 = jnp.pad(p2, ((0, 0), (0, 6), (0, 0)))
    zl = jnp.zeros((depth, HEAD_N, d_r), F32)
    w2bd = jnp.concatenate([jnp.concatenate([decay_w2[:, 0], zl], axis=2),
                            jnp.concatenate([zl, decay_w2[:, 1]], axis=2)], axis=1).astype(BF16)
    a2bd = jnp.concatenate([jnp.concatenate([iclr_a2[:, 0], zl], axis=2),
                            jnp.concatenate([zl, iclr_a2[:, 1]], axis=2)], axis=1).astype(BF16)
    g2p = jnp.pad(gate_g2, ((0, 0), (0, 2 * LANES - d_lg), (0, 0))).astype(BF16)
    lane = jnp.arange(LANES)
    bd = (lane[:, None] // HEAD_N == lane[None, :] // HEAD_N).astype(BF16)
    bd3 = jnp.concatenate([bd, bd, bd], axis=0)
    norm_g3 = norm_g.reshape(depth * 4, 1, d)
    ada_b3 = ada_b.reshape(depth, 1, 6 * d)
    subln3 = diff_subln.reshape(depth, 1, LANES)
    vn3 = chunk_vnorm.reshape(depth, 1, d_c)
    bs_exp = jnp.repeat(jnp.swapaxes(chunk_bs, 1, 2), d_c // n_g, axis=2)
    w_out_b = w_out.astype(BF16)
    rw_p = jnp.pad(router_w, ((0, 0), (0, 0), (0, LANES - n_e)))
    rb_p = jnp.pad(router_b, ((0, 0), (0, LANES - n_e))).reshape(depth, 1, LANES)
    w_gu_b = w_gu.astype(BF16).reshape(depth * n_e, d, -1)
    w_dn_b = w_dn.astype(BF16).reshape(depth * n_e, -1, d)
    b_gu3 = b_gu.reshape(depth * n_e, 1, -1)
    b_dn3 = b_dn.reshape(depth * n_e, 1, d)
    cos_t, sin_t = _rope_tables(s_dec)
    ck4 = cache_k.reshape(b_dec, depth, p_len, d_diff)
    cv4 = cache_v.reshape(b_dec, depth, p_len, d_diff)
    s0_dec = jnp.transpose(state_wkv, (0, 1, 2, 5, 3, 4)).reshape(b_dec, depth, 2, HEAD_N, d_r)
    s0_ctx = jnp.zeros((b_ctx, 2, HEAD_N, d_r), F32)

    cond = jnp.concatenate([c_ctx[None, :], c], axis=0)
    m_rows = -(-cond.shape[0] // 8) * 8
    cond = jnp.pad(cond, ((0, m_rows - cond.shape[0]), (0, 0)))

    x = jnp.concatenate([x_prompt.reshape(t_ctx, d), x_sample.reshape(t_dec, d)], axis=0)
    bm = 512
    ks, vs, ss = [], [], []
    for l in range(depth):
        mods4 = ada_mods(cond, ada_w, ada_b3, l).reshape(m_rows, 6, 1, d)
        z = in_proj(x, mods4, norm_g3, w_in_p, l, t_ctx, s_dec)
        ks.append(z[:t_ctx, (col_q + 1) * d_diff:(col_q + 2) * d_diff])
        vs.append(z[:t_ctx, (col_q + 2) * d_diff:(col_q + 3) * d_diff])

        r, v, kk, w, kd, b, bonus, g = rwkv_prep(z, mu_p, pv, p2, w2bd, a2bd, g2p, bd3, l,
                                                 t_ctx, s_ctx, s_dec, d_r)
        y_c, sf_c = wkv_scan(r, kk, v, w, kd, b, s0_ctx, 0, b_ctx, s_ctx, d_r)
        y_d, _ = wkv_scan(r, kk, v, w, kd, b, s0_dec[:, l], t_ctx, b_dec, s_dec, d_r)
        ss.append(sf_c)
        o_r = rwkv_post(jnp.concatenate([y_c, y_d], axis=1), bonus, g, pv, bd3, l)

        lam_init = 0.8 - 0.6 * math.exp(-0.3 * l)
        lq1, lk1, lq2, lk2 = diff_lam[l].astype(F32)
        lam = (jnp.exp(jnp.sum(lq1 * lk1)) - jnp.exp(jnp.sum(lq2 * lk2)) + lam_init).reshape(1)
        od_c = diff_attn(z, lam, subln3, l, lam_init, 0, b_ctx, s_ctx, d_diff, col_q)
        od_d = diff_attn(z, lam, subln3, l, lam_init, t_ctx, b_dec, s_dec, d_diff, col_q,
                         ctx=(ck4, cv4, cos_t, sin_t))
        o_d = jnp.concatenate([od_c, od_d], axis=0)
        o_c = chunk_mix(z, vn3, chunk_ws, bs_exp, l, d_c, col_q + 3)

        x1, h2, logits = out_proj(x, o_r, o_d, o_c, w_out_b, norm_g3, mods4, rw_p, rb_p, l, t_ctx, s_dec)
        slot_tok, gates, block_e, block_valid, pos = _route(logits[:, :n_e], n_e, bm)
        ys = expert_ffn(h2, slot_tok, block_e, block_valid, w_gu_b, b_gu3, w_dn_b, b_dn3, l, n_e, bm)
        yk = gather_rows(ys, pos.reshape(-1, TOP_K).T.reshape(-1))
        x = moe_combine(x1, yk, gates, norm_g3, mods4, l, t_ctx, s_dec)

    n_hd = cache_k.shape[3]
    new_k = jnp.stack(ks, axis=1).reshape(b_ctx, s_ctx, depth, n_hd, 2, HEAD_N)
    new_k = jnp.transpose(new_k, (0, 2, 1, 3, 4, 5))
    new_v = jnp.stack(vs, axis=1).reshape(b_ctx, s_ctx, depth, n_hd, 2 * HEAD_N)
    new_v = jnp.transpose(new_v, (0, 2, 1, 3, 4))
    st = jnp.stack(ss, axis=1).reshape(b_ctx, depth, 2, HEAD_N, n_h, HEAD_N)
    new_s = jnp.transpose(st, (0, 1, 2, 4, 5, 3))
    y_prompt = x[:t_ctx].reshape(b_ctx, s_ctx, d)
    y_sample = x[t_ctx:].reshape(b_dec, s_dec, d)
    return (y_prompt, y_sample, new_k, new_v, new_s)
```

```python
import functools
import math

import jax
import jax.numpy as jnp
from jax import lax
from jax.experimental import pallas as pl
from jax.experimental.pallas import tpu as pltpu

F32 = jnp.float32
BF16 = jnp.bfloat16

LANES = 128
HEAD_N = 64
NORM_EPS = 1e-6
GN_EPS = 64e-5
ROPE_THETA = 10000.0
GRID_W = 64
TOP_K = 4
SWIGLU_LIMIT = 7.0
SWIGLU_ALPHA = 1.702
CHUNK = 128
VMEM_LIMIT = 56 * 1024 * 1024


def _cparams(sem):
    return pltpu.CompilerParams(dimension_semantics=sem, vmem_limit_bytes=VMEM_LIMIT)


def _split3(x):
    hi = x.astype(BF16)
    r1 = x - hi.astype(F32)
    mid = r1.astype(BF16)
    lo = (r1 - mid.astype(F32)).astype(BF16)
    return hi, mid, lo


def _segsum(x, bd3):
    outs = []
    for j in range(x.shape[1] // LANES):
        hi, mid, lo = _split3(x[:, j * LANES:(j + 1) * LANES])
        lhs = jnp.concatenate([hi, mid, lo], axis=1)
        outs.append(jnp.dot(lhs, bd3, preferred_element_type=F32))
    return outs[0] if len(outs) == 1 else jnp.concatenate(outs, axis=1)


def _rms(x, g):
    return x * lax.rsqrt(jnp.mean(x * x, axis=-1, keepdims=True) + NORM_EPS) * g


def _ada_kernel(c_ref, w_ref, b_ref, o_ref):
    c = c_ref[...]
    s = c * jax.nn.sigmoid(c)
    o_ref[...] = jnp.dot(s.astype(BF16), w_ref[...].astype(BF16),
                         preferred_element_type=F32) + b_ref[...]


def ada_mods(cond, ada_w, ada_b3, l):
    m, d = cond.shape
    n = ada_w.shape[2]
    tn = 1024
    return pl.pallas_call(
        _ada_kernel,
        grid=(n // tn,),
        in_specs=[pl.BlockSpec((m, d), lambda j: (0, 0)),
                  pl.BlockSpec((None, d, tn), lambda j: (l, 0, j)),
                  pl.BlockSpec((None, 1, tn), lambda j: (l, 0, j))],
        out_specs=pl.BlockSpec((m, tn), lambda j: (0, j)),
        out_shape=jax.ShapeDtypeStruct((m, n), F32),
        compiler_params=_cparams(("arbitrary",)),
        name="ada_mods",
    )(cond, ada_w, ada_b3)


def _inproj_kernel(x_ref, g_ref, sh_ref, sc_ref, w_ref, o_ref, h_ref):
    @pl.when(pl.program_id(1) == 0)
    def _():
        h = _rms(x_ref[...], g_ref[...]) * (1.0 + sc_ref[...]) + sh_ref[...]
        h_ref[...] = h.astype(BF16)

    o_ref[...] = jnp.dot(h_ref[...], w_ref[...], preferred_element_type=F32)


def _mod_row(i, tm, t_ctx, s_dec):
    r0 = i * tm
    return jnp.where(r0 < t_ctx, 0, 1 + (r0 - t_ctx) // s_dec)


def in_proj(x, mods4, norm_g3, w_in_p, l, t_ctx, s_dec):
    t, d = x.shape
    n = w_in_p.shape[2]
    tm, tn = math.gcd(1024, t_ctx, s_dec), 512
    mrow = lambda i: _mod_row(i, tm, t_ctx, s_dec)
    return pl.pallas_call(
        _inproj_kernel,
        grid=(t // tm, n // tn),
        in_specs=[pl.BlockSpec((tm, d), lambda i, j: (i, 0)),
                  pl.BlockSpec((None, 1, d), lambda i, j: (4 * l, 0, 0)),
                  pl.BlockSpec((None, None, 1, d), lambda i, j: (mrow(i), 0, 0, 0)),
                  pl.BlockSpec((None, None, 1, d), lambda i, j: (mrow(i), 1, 0, 0)),
                  pl.BlockSpec((None, d, tn), lambda i, j: (l, 0, j))],
        out_specs=pl.BlockSpec((tm, tn), lambda i, j: (i, j)),
        out_shape=jax.ShapeDtypeStruct((t, n), F32),
        scratch_shapes=[pltpu.VMEM((tm, d), BF16)],
        compiler_params=_cparams(("arbitrary", "arbitrary")),
        name="in_proj",
    )(x, norm_g3, mods4, mods4, w_in_p)


def _prep_kernel(z_ref, zp_ref, zn_ref, mu_ref, pv_ref, p2_ref, w2_ref, a2_ref, g2_ref, bd_ref,
                 r_ref, v_ref, kk_ref, w_ref, kd_ref, b_ref, bonus_ref, g_ref, zs_ref,
                 *, tm, t_ctx, s_ctx, s_dec, d_r):
    i = pl.program_id(0)
    r0 = i * tm
    in_ctx = r0 < t_ctx
    pos0 = jnp.where(in_ctx, r0 % s_ctx, (r0 - t_ctx) % s_dec)
    slen = jnp.where(in_ctx, s_ctx, s_dec)
    is_start = pos0 == 0
    is_end = pos0 + tm == slen
    width = z_ref.shape[1]
    rows = lax.broadcasted_iota(jnp.int32, (tm, LANES), 0)
    for j in range(width // LANES):
        sl = slice(j * LANES, (j + 1) * LANES)
        zc = z_ref[:, sl]
        prev_row = jnp.where(is_start, 0.0, zp_ref[7:8, sl])
        next_row = jnp.where(is_end, 0.0, zn_ref[0:1, sl])
        zprev = jnp.where(rows == 0, prev_row, pltpu.roll(zc, 1, axis=0))
        znext = jnp.where(rows == tm - 1, next_row, pltpu.roll(zc, tm - 1, axis=0))
        zs_ref[:, sl] = zc + mu_ref[0:1, sl] * (zprev - zc) + mu_ref[1:2, sl] * (znext - zc)

    bd3 = bd_ref[...]
    r = zs_ref[:, 0:d_r]
    k = zs_ref[:, d_r:2 * d_r]
    v = zs_ref[:, 2 * d_r:3 * d_r]
    o = 3 * d_r
    cw = zs_ref[:, o:o + LANES]
    ca = zs_ref[:, o + LANES:o + 2 * LANES]
    cg = zs_ref[:, o + 2 * LANES:o + 4 * LANES]
    k_k = pv_ref[0:1, :]
    k_a = pv_ref[1:2, :]
    r_k = pv_ref[2:3, :]

    r_ref[...] = r
    v_ref[...] = v
    g_ref[...] = jnp.dot(jax.nn.sigmoid(cg).astype(BF16), g2_ref[...], preferred_element_type=F32)

    kkr = k * k_k
    nrm = jnp.sqrt(_segsum(kkr * kkr, bd3))
    kk = kkr / jnp.maximum(nrm, 1e-12)
    kk_ref[...] = kk

    lw = p2_ref[0:1, :] + jnp.dot(jnp.tanh(cw).astype(BF16), w2_ref[...], preferred_element_type=F32)
    decay = -math.exp(-0.5) * jax.nn.sigmoid(lw)
    a = jax.nn.sigmoid(p2_ref[1:2, :] + jnp.dot(ca.astype(BF16), a2_ref[...], preferred_element_type=F32))
    kd_sum = None
    for d in range(2):
        a_d = a[:, d * d_r:(d + 1) * d_r]
        kd = k * (1.0 + (a_d - 1.0) * k_a)
        w_ref[d] = decay[:, d * d_r:(d + 1) * d_r]
        kd_ref[d] = kd
        b_ref[d] = kk * a_d
        kd_sum = kd if kd_sum is None else kd_sum + kd
    bonus_ref[...] = _segsum(r * kd_sum * r_k, bd3) * v


def rwkv_prep(z, mu_p, pv, p2, w2bd, a2bd, g2p, bd3, l, t_ctx, s_ctx, s_dec, d_r):
    t = z.shape[0]
    tm = 128
    wz = mu_p.shape[2]
    nb8 = t // 8
    kern = functools.partial(_prep_kernel, tm=tm, t_ctx=t_ctx, s_ctx=s_ctx, s_dec=s_dec, d_r=d_r)
    row = pl.BlockSpec((tm, d_r), lambda i: (i, 0))
    row2 = pl.BlockSpec((2, tm, d_r), lambda i: (0, i, 0))
    o1 = jax.ShapeDtypeStruct((t, d_r), F32)
    o2 = jax.ShapeDtypeStruct((2, t, d_r), F32)
    return pl.pallas_call(
        kern,
        grid=(t // tm,),
        in_specs=[pl.BlockSpec((tm, wz), lambda i: (i, 0)),
                  pl.BlockSpec((8, wz), lambda i: (jnp.maximum(i * (tm // 8) - 1, 0), 0)),
                  pl.BlockSpec((8, wz), lambda i: (jnp.minimum((i + 1) * (tm // 8), nb8 - 1), 0)),
                  pl.BlockSpec((None, 8, wz), lambda i: (l, 0, 0)),
                  pl.BlockSpec((None, 8, d_r), lambda i: (l, 0, 0)),
                  pl.BlockSpec((None, 8, 2 * d_r), lambda i: (l, 0, 0)),
                  pl.BlockSpec((None, LANES, 2 * d_r), lambda i: (l, 0, 0)),
                  pl.BlockSpec((None, LANES, 2 * d_r), lambda i: (l, 0, 0)),
                  pl.BlockSpec((None, 2 * LANES, d_r), lambda i: (l, 0, 0)),
                  pl.BlockSpec((3 * LANES, LANES), lambda i: (0, 0))],
        out_specs=[row, row, row, row2, row2, row2, row, row],
        out_shape=[o1, o1, o1, o2, o2, o2, o1, o1],
        scratch_shapes=[pltpu.VMEM((tm, wz), F32)],
        compiler_params=_cparams(("arbitrary",)),
        name="rwkv_prep",
    )(z, z, z, mu_p, pv, p2, w2bd, a2bd, g2p, bd3)


def _split2(x):
    hi = x.astype(BF16)
    lo = (x - hi.astype(F32)).astype(BF16)
    return hi, lo


def _lhs3(a, axis):
    ah, al = _split2(a)
    return jnp.concatenate([ah, ah, al], axis=axis)


def _rhs3(b, axis):
    bh, bl = _split2(b)
    return jnp.concatenate([bh, bl, bh], axis=axis)


def _dot3(l3, r3):
    return jnp.dot(l3, r3, preferred_element_type=F32)


def _dot3_nt(l3, r3):
    return lax.dot_general(l3, r3, (((1,), (1,)), ((), ())), preferred_element_type=F32)


def _dot3_tn(l3, r3):
    return lax.dot_general(l3, r3, (((0,), (0,)), ((), ())), preferred_element_type=F32)


def _mm1(a, b):
    return jnp.dot(a.astype(BF16), b.astype(BF16), preferred_element_type=F32)


SCAN_C = 64


def _scan_kernel(r_ref, kk_ref, v_ref, lw_ref, kd_ref, b_ref, s0_ref,
                 y_ref, sf_ref, s_ref, kr_s, t_s, g4_s, gv_s, kb_s, gc_s, *, tb, nt, nlt):
    c_len = SCAN_C
    fwd = pl.program_id(1) == 0
    tstep = pl.program_id(3)
    lane = lax.broadcasted_iota(jnp.int32, (c_len, LANES), 1)
    row = lax.broadcasted_iota(jnp.int32, (c_len, LANES), 0)
    colm = lane & (HEAD_N - 1)
    ahead = (row - colm) * jnp.where(fwd, 1, -1)
    strict = ahead > 0
    incl = ahead >= 0
    h0 = lane < HEAD_N
    lane2 = lax.broadcasted_iota(jnp.int32, (2 * c_len, LANES), 1)
    row2 = lax.broadcasted_iota(jnp.int32, (2 * c_len, LANES), 0)
    bdmask = (lane2 < HEAD_N) == (row2 < HEAD_N)
    eye = lane2 == row2
    m_incl = jnp.where(jnp.logical_and(incl, h0), 1.0, 0.0).astype(BF16)
    m_incl3 = jnp.concatenate([m_incl, m_incl, m_incl], axis=1)
    zpad = jnp.zeros((c_len, LANES), BF16)

    def two_heads(x):
        return jnp.concatenate([jnp.where(h0, x, 0.0), jnp.where(h0, 0.0, x)], axis=0)

    @pl.when(tstep == 0)
    def _():
        for j in range(nlt):
            s_ref[j] = two_heads(s0_ref[:, j * LANES:(j + 1) * LANES])

    n_ch = tb // c_len

    def chunk_rows(ci):
        cc = jnp.where(fwd, ci, n_ch - 1 - ci)
        return pl.ds(pl.multiple_of(cc * c_len, c_len), c_len)

    tiles = range(nlt)
    lanes_of = [slice(j * LANES, (j + 1) * LANES) for j in tiles]
    eye_f = jnp.where(eye, 1.0, 0.0)
    for ci in range(n_ch):
        rows = chunk_rows(ci)
        lws = [lw_ref[rows, sl] for sl in lanes_of]
        lgs = []
        for lw in lws:
            hi, mid, lo = _split3(lw)
            lgs.append(jnp.dot(m_incl3, jnp.concatenate([hi, zpad, mid, zpad, lo, zpad], axis=0),
                               preferred_element_type=F32))
        lasts = [jnp.where(fwd, lg[c_len - 1:c_len, :], lg[0:1, :]) for lg in lgs]
        kr3s = [_lhs3(jnp.concatenate([kk_ref[rows, sl] * jnp.exp(lg - lw), r_ref[rows, sl] * jnp.exp(lg)],
                                      axis=0), 1)
                for sl, lg, lw in zip(lanes_of, lgs, lws)]
        ens = [jnp.exp(-lg) for lg in lgs]
        gabs = [_dot3_nt(kr3, jnp.concatenate([_rhs3(two_heads(kd_ref[rows, sl] * en), 1),
                                               _rhs3(two_heads(b_ref[rows, sl] * en), 1)], axis=0))
                for kr3, sl, en in zip(kr3s, lanes_of, ens)]
        gas = [gab[:, :LANES] for gab in gabs]
        gbs = [gab[:, LANES:] for gab in gabs]
        ps = [-two_heads(jnp.where(strict, gb[:c_len], 0.0)) for gb in gbs]
        accs = [eye_f + p for p in ps]
        ps = [_mm1(p, p) for p in ps]
        span = 4
        while span < c_len:
            both = [_mm1(jnp.concatenate([p, acc], axis=0), p) for p, acc in zip(ps, accs)]
            accs = [acc + b2[2 * c_len:] for acc, b2 in zip(accs, both)]
            ps = [b2[:2 * c_len] for b2 in both]
            span *= 2
        accs = [acc + _mm1(acc, p) for acc, p in zip(accs, ps)]
        gvs = [_dot3(_lhs3(jnp.concatenate([jnp.where(strict, ga[:c_len], 0.0),
                                            jnp.where(incl, ga[c_len:], 0.0)], axis=0), 1),
                     _rhs3(two_heads(v_ref[rows, sl]), 0))
               for ga, sl in zip(gas, lanes_of)]
        for j in tiles:
            q = ci * nlt + j
            ec = jnp.exp(lasts[j] - lgs[j])
            kr_s[q] = kr3s[j]
            t_s[q] = _lhs3(accs[j][:c_len] + accs[j][c_len:], 1)
            g4_s[q] = _lhs3(jnp.where(incl, gbs[j][c_len:], 0.0), 1)
            gv_s[q] = gvs[j]
            kb_s[q] = _lhs3(jnp.concatenate([kd_ref[rows, lanes_of[j]] * ec, b_ref[rows, lanes_of[j]] * ec],
                                            axis=0), 0)
            gcol = jnp.transpose(jnp.broadcast_to(jnp.exp(lasts[j]), (2 * c_len, LANES)))
            gc_s[q] = jnp.where(bdmask, gcol, 0.0)

    for ci in range(n_ch):
        rows = chunk_rows(ci)
        qs = [ci * nlt + j for j in tiles]
        bdss = [s_ref[j] for j in tiles]
        xrs = [_dot3(kr_s[q], _rhs3(bds, 0)) for q, bds in zip(qs, bdss)]
        us = [_dot3(t_s[q], _rhs3(two_heads(xr[:c_len] + gv_s[q, :c_len]), 0)) for q, xr in zip(qs, xrs)]
        news = [_dot3_tn(kb_s[q], _rhs3(jnp.concatenate([v_ref[rows, sl], -u], axis=0), 0))
                for q, sl, u in zip(qs, lanes_of, us)]
        for j in tiles:
            s_ref[j] = jnp.where(bdmask, news[j], 0.0) + bdss[j] * gc_s[qs[j]]
        for j in tiles:
            y_ref[rows, lanes_of[j]] = (xrs[j][c_len:] + gv_s[qs[j], c_len:]
                                        - _dot3(g4_s[qs[j]], _rhs3(two_heads(us[j]), 0)))

    @pl.when(tstep == nt - 1)
    def _():
        for j in range(nlt):
            bds = s_ref[j]
            sf_ref[:, j * LANES:(j + 1) * LANES] = bds[:HEAD_N] + bds[HEAD_N:]


def wkv_scan(r, kk, v, lw, kd, b, s0, row0, n_seq, s_len, d_r):
    tb = min(256, s_len)
    nt = s_len // tb
    blk0 = row0 // tb
    nlt = 8
    wl = nlt * LANES
    nq = (tb // SCAN_C) * nlt

    def tblk(bi, d, c):
        return blk0 + bi * nt + c + d * (nt - 1 - 2 * c)

    shared = pl.BlockSpec((tb, wl), lambda bi, d, hp, c: (tblk(bi, d, c), hp))
    perdir = pl.BlockSpec((None, tb, wl), lambda bi, d, hp, c: (d, tblk(bi, d, c), hp))
    state = pl.BlockSpec((None, None, HEAD_N, wl), lambda bi, d, hp, c: (bi, d, 0, hp))
    kern = functools.partial(_scan_kernel, tb=tb, nt=nt, nlt=nlt)
    return pl.pallas_call(
        kern,
        grid=(n_seq, 2, d_r // wl, nt),
        in_specs=[shared, shared, shared, perdir, perdir, perdir, state],
        out_specs=[pl.BlockSpec((None, tb, wl),
                                lambda bi, d, hp, c: (d, bi * nt + c + d * (nt - 1 - 2 * c), hp)),
                   state],
        out_shape=[jax.ShapeDtypeStruct((2, n_seq * s_len, d_r), F32),
                   jax.ShapeDtypeStruct((n_seq, 2, HEAD_N, d_r), F32)],
        scratch_shapes=[pltpu.VMEM((nlt, 2 * HEAD_N, LANES), F32),
                        pltpu.VMEM((nq, 2 * SCAN_C, 3 * LANES), BF16),
                        pltpu.VMEM((nq, SCAN_C, 3 * LANES), BF16),
                        pltpu.VMEM((nq, SCAN_C, 3 * LANES), BF16),
                        pltpu.VMEM((nq, 2 * SCAN_C, LANES), F32),
                        pltpu.VMEM((nq, 6 * SCAN_C, LANES), BF16),
                        pltpu.VMEM((nq, 2 * HEAD_N, LANES), F32)],
        compiler_params=_cparams(("arbitrary", "arbitrary", "arbitrary", "arbitrary")),
        name="wkv_scan",
    )(r, kk, v, lw, kd, b, s0)


def _post_kernel(y_ref, bonus_ref, g_ref, pv_ref, bd_ref, o_ref):
    bd3 = bd_ref[...]
    y = y_ref[0] + y_ref[1]
    mu = _segsum(y, bd3) * (1.0 / HEAD_N)
    yc = y - mu
    var = _segsum(yc * yc, bd3) * (1.0 / HEAD_N)
    yn = yc * lax.rsqrt(var + GN_EPS) * pv_ref[3:4, :] + pv_ref[4:5, :]
    o_ref[...] = (yn + bonus_ref[...]) * g_ref[...]


def rwkv_post(y, bonus, g, pv, bd3, l):
    t, d_r = bonus.shape
    tm = 256
    row = pl.BlockSpec((tm, d_r), lambda i: (i, 0))
    return pl.pallas_call(
        _post_kernel,
        grid=(t // tm,),
        in_specs=[pl.BlockSpec((2, tm, d_r), lambda i: (0, i, 0)), row, row,
                  pl.BlockSpec((None, 8, d_r), lambda i: (l, 0, 0)),
                  pl.BlockSpec((3 * LANES, LANES), lambda i: (0, 0))],
        out_specs=row,
        out_shape=jax.ShapeDtypeStruct((t, d_r), F32),
        compiler_params=_cparams(("arbitrary",)),
        name="rwkv_post",
    )(y, bonus, g, pv, bd3)


def _rope(x, cos, sin):
    lane = lax.broadcasted_iota(jnp.int32, (x.shape[0], LANES), 1)
    first = (lane & 31) < 16
    outs = []
    for j in range(x.shape[1] // LANES):
        xc = x[:, j * LANES:(j + 1) * LANES]
        partner = jnp.where(first, pltpu.roll(xc, LANES - 16, axis=1), pltpu.roll(xc, 16, axis=1))
        outs.append(xc * cos + partner * sin)
    return outs[0] if len(outs) == 1 else jnp.concatenate(outs, axis=1)


def _attn_kernel(lam_ref, q_ref, k_ref, v_ref, *rest, use_ctx, n_heads, out_scale):
    if use_ctx:
        (ck_ref, cv_ref, cosq_ref, sinq_ref, cosk_ref, sink_ref, g_ref,
         o_ref, kb_ref, vb_ref, ckb_ref, cvb_ref) = rest
    else:
        g_ref, o_ref, kb_ref, vb_ref = rest
    qi = pl.program_id(1)

    @pl.when(qi == 0)
    def _():
        k = k_ref[...]
        if use_ctx:
            k = _rope(k, cosk_ref[...], sink_ref[...])
            ckb_ref[...] = ck_ref[...].astype(BF16)
            cvb_ref[...] = cv_ref[...].astype(BF16)
        kb_ref[...] = k.astype(BF16)
        vb_ref[...] = v_ref[...].astype(BF16)

    lam = lam_ref[0]
    q = q_ref[...]
    if use_ctx:
        q = _rope(q, cosq_ref[...], sinq_ref[...])
    tq = q.shape[0]
    lane = lax.broadcasted_iota(jnp.int32, (tq, LANES), 1)
    scale = HEAD_N ** -0.5
    dn = (((1,), (1,)), ((), ()))
    for h in range(n_heads):
        sl = slice(h * LANES, (h + 1) * LANES)
        qh = q[:, sl] * scale
        kh = kb_ref[:, sl]
        outs = []
        for comp in range(2):
            qc = jnp.where((lane < HEAD_N) == (comp == 0), qh, 0.0).astype(BF16)
            s_a = lax.dot_general(qc, kh, dn, preferred_element_type=F32)
            m = jnp.max(s_a, axis=-1, keepdims=True)
            if use_ctx:
                s_b = lax.dot_general(qc, ckb_ref[:, sl], dn, preferred_element_type=F32)
                m = jnp.maximum(m, jnp.max(s_b, axis=-1, keepdims=True))
            e_a = jnp.exp(s_a - m)
            den = jnp.sum(e_a, axis=-1, keepdims=True)
            acc = jnp.dot(e_a.astype(BF16), vb_ref[:, sl], preferred_element_type=F32)
            if use_ctx:
                e_b = jnp.exp(s_b - m)
                den = den + jnp.sum(e_b, axis=-1, keepdims=True)
                acc = acc + jnp.dot(e_b.astype(BF16), cvb_ref[:, sl], preferred_element_type=F32)
            outs.append(acc / den)
        o = outs[0] - lam * outs[1]
        o_ref[:, sl] = _rms(o, g_ref[...]) * out_scale


def diff_attn(z, lam, subln3, l, lam_init, row0, n_seq, s_len, d_diff, col0, ctx=None):
    tq = 256
    nq = s_len // tq
    n_heads = d_diff // LANES
    qb0 = row0 // tq
    sb0 = row0 // s_len
    use_ctx = ctx is not None
    in_specs = [pl.BlockSpec(memory_space=pltpu.SMEM),
                pl.BlockSpec((tq, d_diff), lambda bi, qi: (qb0 + bi * nq + qi, col0)),
                pl.BlockSpec((s_len, d_diff), lambda bi, qi: (sb0 + bi, col0 + 1)),
                pl.BlockSpec((s_len, d_diff), lambda bi, qi: (sb0 + bi, col0 + 2))]
    args = [lam, z, z, z]
    scratch = [pltpu.VMEM((s_len, d_diff), BF16), pltpu.VMEM((s_len, d_diff), BF16)]
    if use_ctx:
        cache_k, cache_v, cos, sin = ctx
        p_len = cache_k.shape[2]
        cspec = pl.BlockSpec((None, None, p_len, d_diff), lambda bi, qi: (bi, l, 0, 0))
        in_specs += [cspec, cspec,
                     pl.BlockSpec((tq, LANES), lambda bi, qi: (qi, 0)),
                     pl.BlockSpec((tq, LANES), lambda bi, qi: (qi, 0)),
                     pl.BlockSpec((s_len, LANES), lambda bi, qi: (0, 0)),
                     pl.BlockSpec((s_len, LANES), lambda bi, qi: (0, 0))]
        args += [cache_k, cache_v, cos, sin, cos, sin]
        scratch += [pltpu.VMEM((p_len, d_diff), BF16), pltpu.VMEM((p_len, d_diff), BF16)]
    in_specs.append(pl.BlockSpec((None, 1, LANES), lambda bi, qi: (l, 0, 0)))
    args.append(subln3)
    kern = functools.partial(_attn_kernel, use_ctx=use_ctx, n_heads=n_heads,
                             out_scale=1.0 - lam_init)
    return pl.pallas_call(
        kern,
        grid=(n_seq, nq),
        in_specs=in_specs,
        out_specs=pl.BlockSpec((tq, d_diff), lambda bi, qi: (bi * nq + qi, 0)),
        out_shape=jax.ShapeDtypeStruct((n_seq * s_len, d_diff), F32),
        scratch_shapes=scratch,
        compiler_params=_cparams(("arbitrary", "arbitrary")),
        name="diff_attn_ctx" if use_ctx else "diff_attn",
    )(*args)


def _chunk_kernel(u_ref, g_ref, vn_ref, ws_ref, bs_ref, o_ref):
    u = jax.nn.gelu(u_ref[...], approximate=True)
    vv = _rms(jax.nn.gelu(g_ref[...], approximate=True), vn_ref[...])
    for gi in range(ws_ref.shape[0]):
        sl = slice(gi * LANES, (gi + 1) * LANES)
        s = jnp.dot(ws_ref[gi].astype(BF16), vv[:, sl].astype(BF16), preferred_element_type=F32)
        o_ref[:, sl] = u[:, sl] * (s + bs_ref[:, sl])


def chunk_mix(z, vn3, chunk_ws, bs_exp, l, d_c, col_u):
    t = z.shape[0]
    n_g = chunk_ws.shape[1]
    return pl.pallas_call(
        _chunk_kernel,
        grid=(t // CHUNK,),
        in_specs=[pl.BlockSpec((CHUNK, d_c), lambda i: (i, col_u)),
                  pl.BlockSpec((CHUNK, d_c), lambda i: (i, col_u + 1)),
                  pl.BlockSpec((None, 1, d_c), lambda i: (l, 0, 0)),
                  pl.BlockSpec((None, n_g, CHUNK, CHUNK), lambda i: (l, 0, 0, 0)),
                  pl.BlockSpec((None, CHUNK, d_c), lambda i: (l, 0, 0))],
        out_specs=pl.BlockSpec((CHUNK, d_c), lambda i: (i, 0)),
        out_shape=jax.ShapeDtypeStruct((t, d_c), F32),
        compiler_params=_cparams(("arbitrary",)),
        name="chunk_mix",
    )(z, z, vn3, chunk_ws, bs_exp)


def _outproj_kernel(x_ref, or_ref, od_ref, oc_ref, wr_ref, wd_ref, wc_ref, g1n_ref, g2n_ref,
                    gate_ref, sh_ref, sc_ref, rw_ref, rb_ref, x1_ref, h2_ref, lg_ref):
    mix = jnp.dot(or_ref[...].astype(BF16), wr_ref[...], preferred_element_type=F32)
    mix = mix + jnp.dot(od_ref[...].astype(BF16), wd_ref[...], preferred_element_type=F32)
    mix = mix + jnp.dot(oc_ref[...].astype(BF16), wc_ref[...], preferred_element_type=F32)
    x1 = x_ref[...] + gate_ref[...] * _rms(mix, g1n_ref[...])
    x1_ref[...] = x1
    h2 = _rms(x1, g2n_ref[...]) * (1.0 + sc_ref[...]) + sh_ref[...]
    half = h2.shape[1] // 2
    bits = lax.bitcast_convert_type(h2.astype(BF16).astype(F32), jnp.uint32)
    h2_ref[...] = (bits[:, :half] >> 16) | bits[:, half:]
    lg_ref[...] = _dot3(_lhs3(h2, 1), _rhs3(rw_ref[...], 0)) + rb_ref[...]


def out_proj(x, o_r, o_d, o_c, w_out_b, norm_g3, mods4, rw_p, rb_p, l, t_ctx, s_dec):
    t, d = x.shape
    d_r, d_d, d_c = o_r.shape[1], o_d.shape[1], o_c.shape[1]
    tm = math.gcd(256, t_ctx, s_dec)
    mrow = lambda i: _mod_row(i, tm, t_ctx, s_dec)
    nrm = lambda k: pl.BlockSpec((None, 1, d), lambda i: (4 * l + k, 0, 0))
    mod = lambda k: pl.BlockSpec((None, None, 1, d), lambda i: (mrow(i), k, 0, 0))
    row = lambda w: pl.BlockSpec((tm, w), lambda i: (i, 0))
    nrb = d_r // d_d
    return pl.pallas_call(
        _outproj_kernel,
        grid=(t // tm,),
        in_specs=[row(d), row(d_r), row(d_d), row(d_c),
                  pl.BlockSpec((None, d_r, d), lambda i: (l, 0, 0)),
                  pl.BlockSpec((None, d_d, d), lambda i: (l, nrb, 0)),
                  pl.BlockSpec((None, d_c, d), lambda i: (l, nrb + 1, 0)),
                  nrm(1), nrm(2), mod(2), mod(3), mod(4),
                  pl.BlockSpec((None, d, LANES), lambda i: (l, 0, 0)),
                  pl.BlockSpec((None, 1, LANES), lambda i: (l, 0, 0))],
        out_specs=[row(d), row(d // 2), row(LANES)],
        out_shape=[jax.ShapeDtypeStruct((t, d), F32), jax.ShapeDtypeStruct((t, d // 2), jnp.uint32),
                   jax.ShapeDtypeStruct((t, LANES), F32)],
        compiler_params=_cparams(("arbitrary",)),
        name="out_proj",
    )(x, o_r, o_d, o_c, w_out_b, w_out_b, w_out_b, norm_g3, norm_g3, mods4, mods4, mods4, rw_p, rb_p)


def _gather_kernel(idx_ref, src_ref, o_ref, sem, *, bm):
    def row_copy(r, src_row):
        return pltpu.make_async_copy(src_ref.at[pl.ds(src_row, 1), :], o_ref.at[pl.ds(r, 1), :], sem)

    def issue(i, carry):
        r = 2 * i
        row_copy(r, idx_ref[0, r]).start(priority=0)
        row_copy(r + 1, idx_ref[0, r + 1]).start(priority=1)
        return carry

    lax.fori_loop(0, bm // 2, issue, 0)
    pltpu.make_async_copy(o_ref, o_ref, sem).wait()


def gather_rows(src, idx, bm=512):
    n, d = src.shape
    m = idx.shape[0]
    return pl.pallas_call(
        functools.partial(_gather_kernel, bm=bm),
        grid=(m // bm,),
        in_specs=[pl.BlockSpec((None, 1, bm), lambda i: (i, 0, 0), memory_space=pltpu.SMEM),
                  pl.BlockSpec(memory_space=pl.ANY)],
        out_specs=pl.BlockSpec((bm, d), lambda i: (i, 0)),
        out_shape=jax.ShapeDtypeStruct((m, d), src.dtype),
        scratch_shapes=[pltpu.SemaphoreType.DMA],
        compiler_params=_cparams(("arbitrary",)),
        name="gather_rows",
    )(idx.reshape(m // bm, 1, bm), src)


def _expert_kernel(be_ref, bv_ref, idx_ref, nidx_ref, h2_ref, wg_ref, wu_ref, bg_ref, bu_ref, wd_ref,
                   bdn_ref, o_ref, xb_ref, xg_ref, sem, *, nf, nb, bm):
    i = pl.program_id(0)
    f = pl.program_id(1)
    slot = i % 2
    per_step = bm // nf

    def row_copy(src_idx_ref, row, s):
        return pltpu.make_async_copy(h2_ref.at[pl.ds(src_idx_ref[0, row], 1), :],
                                     xg_ref.at[s, pl.ds(row, 1), :], sem.at[s])

    def block_wait(s):
        pltpu.make_async_copy(xg_ref.at[s], xg_ref.at[s], sem.at[s]).wait()

    @pl.when(jnp.logical_and(i == 0, f == 0))
    def _():
        def issue(j, carry):
            row_copy(idx_ref, 2 * j, 0).start(priority=0)
            row_copy(idx_ref, 2 * j + 1, 0).start(priority=1)
            return carry
        lax.fori_loop(0, bm // 2, issue, 0)

    @pl.when(f == 0)
    def _():
        block_wait(slot)
        p = xg_ref[slot]
        half = p.shape[1]
        xb_ref[:, :half] = lax.bitcast_convert_type(p << 16, F32).astype(BF16)
        xb_ref[:, half:] = lax.bitcast_convert_type(p & jnp.uint32(0xFFFF0000), F32).astype(BF16)
        o_ref[...] = jnp.zeros_like(o_ref)

    row0 = f * per_step

    @pl.when(bv_ref[i] > 0)
    def _():
        for r in range(per_step):
            row_copy(nidx_ref, row0 + r, 1 - slot).start(priority=r % 2)
        xb = xb_ref[...]
        gt = jnp.dot(xb, wg_ref[...], preferred_element_type=F32) + bg_ref[...]
        up = jnp.dot(xb, wu_ref[...], preferred_element_type=F32) + bu_ref[...]
        gt = jnp.minimum(gt, SWIGLU_LIMIT)
        up = jnp.clip(up, -SWIGLU_LIMIT, SWIGLU_LIMIT)
        act = (up + 1.0) * gt * jax.nn.sigmoid(SWIGLU_ALPHA * gt)
        o_ref[...] += jnp.dot(act.astype(BF16), wd_ref[...], preferred_element_type=F32)

    @pl.when(bv_ref[i] <= 0)
    def _():
        def issue(j, carry):
            row_copy(nidx_ref, row0 + 2 * j, 1 - slot).start(priority=0)
            row_copy(nidx_ref, row0 + 2 * j + 1, 1 - slot).start(priority=1)
            return carry
        lax.fori_loop(0, per_step // 2, issue, 0)

    @pl.when(f == nf - 1)
    def _():
        o_ref[...] = o_ref[...] + bdn_ref[...]

    @pl.when(jnp.logical_and(i == nb - 1, f == nf - 1))
    def _():
        block_wait(1 - slot)


def expert_ffn(h2p, slot_tok, block_e, block_valid, w_gu_b, b_gu3, w_dn_b, b_dn3, l, n_e, bm):
    d_half = h2p.shape[1]
    n_slots = slot_tok.shape[0]
    d = 2 * d_half
    ff = w_dn_b.shape[1]
    ft = 512
    nf = ff // ft
    nb = n_slots // bm
    e0 = l * n_e
    idx3 = slot_tok.reshape(nb, 1, bm)

    def fsel(i, f, bv):
        return jnp.where(bv[i] > 0, f, nf - 1)

    return pl.pallas_call(
        functools.partial(_expert_kernel, nf=nf, nb=nb, bm=bm),
        grid_spec=pltpu.PrefetchScalarGridSpec(
            num_scalar_prefetch=2,
            grid=(nb, nf),
            in_specs=[pl.BlockSpec((None, 1, bm), lambda i, f, be, bv: (i, 0, 0), memory_space=pltpu.SMEM),
                      pl.BlockSpec((None, 1, bm), lambda i, f, be, bv: (jnp.minimum(i + 1, nb - 1), 0, 0),
                                   memory_space=pltpu.SMEM),
                      pl.BlockSpec(memory_space=pl.ANY),
                      pl.BlockSpec((None, d, ft), lambda i, f, be, bv: (e0 + be[i], 0, fsel(i, f, bv))),
                      pl.BlockSpec((None, d, ft), lambda i, f, be, bv: (e0 + be[i], 0, nf + fsel(i, f, bv))),
                      pl.BlockSpec((None, 1, ft), lambda i, f, be, bv: (e0 + be[i], 0, fsel(i, f, bv))),
                      pl.BlockSpec((None, 1, ft), lambda i, f, be, bv: (e0 + be[i], 0, nf + fsel(i, f, bv))),
                      pl.BlockSpec((None, ft, d), lambda i, f, be, bv: (e0 + be[i], fsel(i, f, bv), 0)),
                      pl.BlockSpec((None, 1, d), lambda i, f, be, bv: (e0 + be[i], 0, 0))],
            out_specs=pl.BlockSpec((bm, d), lambda i, f, be, bv: (i, 0)),
            scratch_shapes=[pltpu.VMEM((bm, d), BF16),
                            pltpu.VMEM((2, bm, d_half), jnp.uint32),
                            pltpu.SemaphoreType.DMA((2,))]),
        out_shape=jax.ShapeDtypeStruct((n_slots, d), F32),
        compiler_params=_cparams(("arbitrary", "arbitrary")),
        name="expert_ffn",
    )(block_e, block_valid, idx3, idx3, h2p, w_gu_b, w_gu_b, b_gu3, b_gu3, w_dn_b, b_dn3)


def _combine_kernel(x_ref, *rest):
    y_refs = rest[:TOP_K]
    rg_ref, gn_ref, gate_ref, o_ref = rest[TOP_K:]
    f = y_refs[0][...] * rg_ref[:, 0:1]
    for k in range(1, TOP_K):
        f = f + y_refs[k][...] * rg_ref[:, k:k + 1]
    o_ref[...] = x_ref[...] + gate_ref[...] * _rms(f, gn_ref[...])


def moe_combine(x1, yk, gates, norm_g3, mods4, l, t_ctx, s_dec):
    t, d = x1.shape
    tm = 256
    nblk = t // tm
    mrow = lambda i: _mod_row(i, tm, t_ctx, s_dec)
    y_specs = [pl.BlockSpec((tm, d), functools.partial(lambda i, k: (k * nblk + i, 0), k=k))
               for k in range(TOP_K)]
    return pl.pallas_call(
        _combine_kernel,
        grid=(nblk,),
        in_specs=[pl.BlockSpec((tm, d), lambda i: (i, 0)),
                  *y_specs,
                  pl.BlockSpec((tm, TOP_K), lambda i: (i, 0)),
                  pl.BlockSpec((None, 1, d), lambda i: (4 * l + 3, 0, 0)),
                  pl.BlockSpec((None, None, 1, d), lambda i: (mrow(i), 5, 0, 0))],
        out_specs=pl.BlockSpec((tm, d), lambda i: (i, 0)),
        out_shape=jax.ShapeDtypeStruct((t, d), F32),
        compiler_params=_cparams(("arbitrary",)),
        name="moe_combine",
    )(x1, *([yk] * TOP_K), gates, norm_g3, mods4)


def _route(logits, n_e, bm):
    t = logits.shape[0]
    top_v, top_i = lax.top_k(logits, TOP_K)
    gates = jax.nn.softmax(top_v, axis=-1)
    tk = t * TOP_K
    flat_e = top_i.reshape(-1).astype(jnp.int32)
    eids = jnp.arange(n_e, dtype=jnp.int32)
    iota = jnp.arange(tk, dtype=jnp.int32)
    sorted_e, order = lax.sort_key_val(flat_e, iota)
    counts = jnp.sum((flat_e[:, None] == eids[None, :]).astype(jnp.int32), axis=0)
    padded = ((counts + bm - 1) // bm) * bm
    pad_end = jnp.cumsum(padded)
    pad_start = pad_end - padded
    start = jnp.cumsum(counts) - counts
    shift = pad_start - start
    dest = iota + jnp.sum(jnp.where(sorted_e[:, None] == eids[None, :], shift[None, :], 0), axis=1)
    _, pos = lax.sort_key_val(order, dest)
    nb = (tk + n_e * (bm - 1) + bm - 1) // bm
    starts = jnp.arange(nb, dtype=jnp.int32) * bm
    block_e = jnp.minimum(jnp.sum((pad_end[None, :] <= starts[:, None]).astype(jnp.int32), axis=1), n_e - 1)
    block_valid = (starts < pad_end[-1]).astype(jnp.int32)
    src = starts[:, None] + jnp.arange(bm, dtype=jnp.int32)[None, :] - shift[block_e][:, None]
    live = jnp.logical_and(src < (start + counts)[block_e][:, None], block_valid[:, None] > 0)
    slot_tok = jnp.where(live, order[jnp.clip(src, 0, tk - 1)] // TOP_K, 0).reshape(-1)
    last_e = jnp.max(jnp.where(counts > 0, eids, 0))
    block_e = jnp.where(block_valid > 0, block_e, last_e)
    return slot_tok, gates, block_e, block_valid, pos


def _rope_tables(s_len):
    rows = s_len // GRID_W
    row = jnp.repeat(jnp.arange(rows), GRID_W)
    col = jnp.tile(jnp.arange(GRID_W), rows)
    half = HEAD_N // 2
    inv_freq = 1.0 / (ROPE_THETA ** (jnp.arange(0, half, 2, dtype=F32) / half))
    ang = jnp.stack([row, col], axis=-1).astype(F32)[..., None] * inv_freq
    cos, sin = jnp.cos(ang), jnp.sin(ang)
    cos64 = jnp.concatenate([cos[:, 0], cos[:, 0], cos[:, 1], cos[:, 1]], axis=-1)
    sin64 = jnp.concatenate([-sin[:, 0], sin[:, 0], -sin[:, 1], sin[:, 1]], axis=-1)
    return jnp.tile(cos64, (1, 2)), jnp.tile(sin64, (1, 2))


def kernel(x_prompt, x_sample, c, cache_k, cache_v, state_wkv, c_ctx, ada_w, ada_b, norm_g, w_in, shift_mu, decay_w0, decay_w2, iclr_a0, iclr_a2, gate_g2, k_k, k_a, r_k, ln_x_w, ln_x_b, diff_lam, diff_subln, chunk_vnorm, chunk_ws, chunk_bs, w_out, router_w, router_b, w_gu, b_gu, w_dn, b_dn):
    b_ctx, s_ctx, d = x_prompt.shape
    b_dec, s_dec, _ = x_sample.shape
    depth = ada_w.shape[0]
    t_ctx, t_dec = b_ctx * s_ctx, b_dec * s_dec
    d_r = k_k.shape[1]
    n_h = d_r // HEAD_N
    d_rin = shift_mu.shape[2]
    d_diff = diff_subln.shape[1] * (cache_k.shape[3])
    n_g = chunk_ws.shape[1]
    d_c = n_g * chunk_ws.shape[2]
    n_e = router_w.shape[2]
    d_lw, d_la, d_lg = decay_w2.shape[2], iclr_a2.shape[2], gate_g2.shape[1]
    p_len = cache_k.shape[2]
    assert d_lw == HEAD_N and d_la == HEAD_N and d_lg <= 2 * LANES and d_diff == d_c
    wz = 3 * d_r + 4 * LANES
    col_q = wz // d_diff
    assert wz % d_diff == 0 and d_rin <= wz

    padc = wz - d_rin
    w_in_p = jnp.concatenate([w_in[:, :, :d_rin], jnp.zeros((depth, d, padc), F32), w_in[:, :, d_rin:]],
                             axis=2).astype(BF16)
    mu_p = jnp.pad(shift_mu, ((0, 0), (0, 6), (0, padc)))
    pv = jnp.stack([k_k, k_a, r_k.reshape(depth, d_r), ln_x_w, ln_x_b], axis=1)
    pv = jnp.pad(pv, ((0, 0), (0, 3), (0, 0)))
    p2 = jnp.stack([decay_w0.reshape(depth, 2 * d_r), iclr_a0.reshape(depth, 2 * d_r)], axis=1)
    p2 = jnp.pad(p2, ((0, 0), (0, 6), (0, 0)))
    zl = jnp.zeros((depth, HEAD_N, d_r), F32)
    w2bd = jnp.concatenate([jnp.concatenate([decay_w2[:, 0], zl], axis=2),
                            jnp.concatenate([zl, decay_w2[:, 1]], axis=2)], axis=1).astype(BF16)
    a2bd = jnp.concatenate([jnp.concatenate([iclr_a2[:, 0], zl], axis=2),
                            jnp.concatenate([zl, iclr_a2[:, 1]], axis=2)], axis=1).astype(BF16)
    g2p = jnp.pad(gate_g2, ((0, 0), (0, 2 * LANES - d_lg), (0, 0))).astype(BF16)
    lane = jnp.arange(LANES)
    bd = (lane[:, None] // HEAD_N == lane[None, :] // HEAD_N).astype(BF16)
    bd3 = jnp.concatenate([bd, bd, bd], axis=0)
    norm_g3 = norm_g.reshape(depth * 4, 1, d)
    ada_b3 = ada_b.reshape(depth, 1, 6 * d)
    subln3 = diff_subln.reshape(depth, 1, LANES)
    vn3 = chunk_vnorm.reshape(depth, 1, d_c)
    bs_exp = jnp.repeat(jnp.swapaxes(chunk_bs, 1, 2), d_c // n_g, axis=2)
    w_out_b = w_out.astype(BF16)
    rw_p = jnp.pad(router_w, ((0, 0), (0, 0), (0, LANES - n_e)))
    rb_p = jnp.pad(router_b, ((0, 0), (0, LANES - n_e))).reshape(depth, 1, LANES)
    w_gu_b = w_gu.astype(BF16).reshape(depth * n_e, d, -1)
    w_dn_b = w_dn.astype(BF16).reshape(depth * n_e, -1, d)
    b_gu3 = b_gu.reshape(depth * n_e, 1, -1)
    b_dn3 = b_dn.reshape(depth * n_e, 1, d)
    cos_t, sin_t = _rope_tables(s_dec)
    ck4 = cache_k.reshape(b_dec, depth, p_len, d_diff)
    cv4 = cache_v.reshape(b_dec, depth, p_len, d_diff)
    s0_dec = jnp.transpose(state_wkv, (0, 1, 2, 5, 3, 4)).reshape(b_dec, depth, 2, HEAD_N, d_r)
    s0_ctx = jnp.zeros((b_ctx, 2, HEAD_N, d_r), F32)

    cond = jnp.concatenate([c_ctx[None, :], c], axis=0)
    m_rows = -(-cond.shape[0] // 8) * 8
    cond = jnp.pad(cond, ((0, m_rows - cond.shape[0]), (0, 0)))

    x = jnp.concatenate([x_prompt.reshape(t_ctx, d), x_sample.reshape(t_dec, d)], axis=0)
    bm = 1024
    ks, vs, ss = [], [], []
    for l in range(depth):
        mods4 = ada_mods(cond, ada_w, ada_b3, l).reshape(m_rows, 6, 1, d)
        z = in_proj(x, mods4, norm_g3, w_in_p, l, t_ctx, s_dec)
        ks.append(z[:t_ctx, (col_q + 1) * d_diff:(col_q + 2) * d_diff])
        vs.append(z[:t_ctx, (col_q + 2) * d_diff:(col_q + 3) * d_diff])

        r, v, kk, w, kd, b, bonus, g = rwkv_prep(z, mu_p, pv, p2, w2bd, a2bd, g2p, bd3, l,
                                                 t_ctx, s_ctx, s_dec, d_r)
        y_c, sf_c = wkv_scan(r, kk, v, w, kd, b, s0_ctx, 0, b_ctx, s_ctx, d_r)
        y_d, _ = wkv_scan(r, kk, v, w, kd, b, s0_dec[:, l], t_ctx, b_dec, s_dec, d_r)
        ss.append(sf_c)
        o_r = rwkv_post(jnp.concatenate([y_c, y_d], axis=1), bonus, g, pv, bd3, l)

        lam_init = 0.8 - 0.6 * math.exp(-0.3 * l)
        lq1, lk1, lq2, lk2 = diff_lam[l].astype(F32)
        lam = (jnp.exp(jnp.sum(lq1 * lk1)) - jnp.exp(jnp.sum(lq2 * lk2)) + lam_init).reshape(1)
        od_c = diff_attn(z, lam, subln3, l, lam_init, 0, b_ctx, s_ctx, d_diff, col_q)
        od_d = diff_attn(z, lam, subln3, l, lam_init, t_ctx, b_dec, s_dec, d_diff, col_q,
                         ctx=(ck4, cv4, cos_t, sin_t))
        o_d = jnp.concatenate([od_c, od_d], axis=0)
        o_c = chunk_mix(z, vn3, chunk_ws, bs_exp, l, d_c, col_q + 3)

        x1, h2, logits = out_proj(x, o_r, o_d, o_c, w_out_b, norm_g3, mods4, rw_p, rb_p, l, t_ctx, s_dec)
        slot_tok, gates, block_e, block_valid, pos = _route(logits[:, :n_e], n_e, bm)
        ys = expert_ffn(h2, slot_tok, block_e, block_valid, w_gu_b, b_gu3, w_dn_b, b_dn3, l, n_e, bm)
        yk = gather_rows(ys, pos.reshape(-1, TOP_K).T.reshape(-1))
        x = moe_combine(x1, yk, gates, norm_g3, mods4, l, t_ctx, s_dec)

    n_hd = cache_k.shape[3]
    new_k = jnp.stack(ks, axis=1).reshape(b_ctx, s_ctx, depth, n_hd, 2, HEAD_N)
    new_k = jnp.transpose(new_k, (0, 2, 1, 3, 4, 5))
    new_v = jnp.stack(vs, axis=1).reshape(b_ctx, s_ctx, depth, n_hd, 2 * HEAD_N)
    new_v = jnp.transpose(new_v, (0, 2, 1, 3, 4))
    st = jnp.stack(ss, axis=1).reshape(b_ctx, depth, 2, HEAD_N, n_h, HEAD_N)
    new_s = jnp.transpose(st, (0, 1, 2, 4, 5, 3))
    y_prompt = x[:t_ctx].reshape(b_ctx, s_ctx, d)
    y_sample = x[t_ctx:].reshape(b_dec, s_dec, d)
    return (y_prompt, y_sample, new_k, new_v, new_s)
```
